```python
import math
import jax, jax.numpy as jnp
from jax import lax
import numpy as np

D_MODEL = 1024
BATCH = 2
SEQ = 8192
DEPTH = 2
DEC_BATCH = 8
DEC_SEQ = 8192
PAST_LEN = 128

GRID_W = 64
MEM_LEN = 256
Q_BLOCK = 128
EPS = 1e-6
ROPE_THETA = 500000.0
MLA_THETA = 10000.0
AXIAL_THETA = 10000.0

A_HEADS = 4
A_QK = 32
A_V = 2 * A_QK
A_ROT = A_QK // 4
B_HEADS = 6
B_NOPE = 64
B_ROPE = 32
B_V = 64
Q_LORA = 256
KV_LORA = 128
C_HEADS = 6
C_KV_HEADS = 2
C_HD = 64

A_WIDTH = A_HEADS * A_V
B_WIDTH = B_HEADS * B_V
C_WIDTH = C_HEADS * C_HD
D_MIX = A_WIDTH + B_WIDTH + C_WIDTH

IN_SIZES = (A_HEADS * 2 * A_QK,
            A_HEADS * 2 * A_QK,
            A_WIDTH,
            Q_LORA,
            KV_LORA,
            B_ROPE,
            C_HEADS * C_HD,
            C_KV_HEADS * C_HD,
            C_KV_HEADS * C_HD)
IN_COLS = sum(IN_SIZES)

X_HEADS = 4
X_HD = 128
X_WIDTH = X_HEADS * X_HD

D_FF = -(-8 * D_MODEL // (3 * 256)) * 256

kernel_name = "hybrid_diff_mla_axialgqa_encoder"


def rms_norm(x, g):
    xf = x.astype(jnp.float32)
    y = xf * lax.rsqrt(jnp.mean(xf * xf, axis=-1, keepdims=True) + EPS)
    return (y * g.astype(jnp.float32)).astype(x.dtype)


def rope(x, pos, theta):
    d = x.shape[-1]
    half = d // 2
    inv = theta ** (-jnp.arange(half, dtype=jnp.float32) * (2.0 / d))
    ang = pos[:, None] * inv[None, :]
    bshape = (ang.shape[0],) + (1,) * (x.ndim - 3) + (half,)
    cos = jnp.cos(ang).reshape(bshape)
    sin = jnp.sin(ang).reshape(bshape)
    xf = x.astype(jnp.float32)
    x1, x2 = xf[..., :half], xf[..., half:]
    return jnp.concatenate([x1 * cos - x2 * sin, x1 * sin + x2 * cos], axis=-1).astype(x.dtype)


def partial_rope(x, pos, n_rot, theta):
    return jnp.concatenate([rope(x[..., :n_rot], pos, theta), x[..., n_rot:]], axis=-1)


def to_blocks(t):
    b, s = t.shape[:2]
    return t.reshape((b, s // Q_BLOCK, Q_BLOCK) + t.shape[2:]).swapaxes(0, 1)


def from_blocks(t):
    nb, b, qb = t.shape[:3]
    return t.swapaxes(0, 1).reshape((b, nb * qb) + t.shape[3:])


def gqa_blocked(q, k, v, scale):
    b, s, h, d = q.shape
    g = k.shape[2]
    r = h // g
    qb = to_blocks(q.reshape(b, s, g, r, d))

    def one(qi):
        sc = jnp.einsum('bqgrd,bkgd->bgrqk', qi, k).astype(jnp.float32) * scale
        p = jax.nn.softmax(sc, axis=-1).astype(v.dtype)
        return jnp.einsum('bgrqk,bkgd->bqgrd', p, v)

    o = from_blocks(lax.map(one, qb))
    return o.reshape(b, s, h, v.shape[-1])


def diff_blocked(q1, q2, k1, k2, v, lam, scale):
    def one(qs):
        a, c = qs
        s1 = jnp.einsum('bqhd,bkhd->bhqk', a, k1).astype(jnp.float32) * scale
        s2 = jnp.einsum('bqhd,bkhd->bhqk', c, k2).astype(jnp.float32) * scale
        p = jax.nn.softmax(s1, axis=-1) - lam * jax.nn.softmax(s2, axis=-1)
        return jnp.einsum('bhqk,bkhd->bqhd', p.astype(v.dtype), v)

    return from_blocks(lax.map(one, (to_blocks(q1), to_blocks(q2))))


def encoder(x, mem, norm_mix, w_in, lam_q1, lam_k1, lam_q2, lam_k2, subln_a,
            mla_q_norm, w_uq, mla_kv_norm, w_ukv, c_q_norm, c_k_norm, w_o,
            norm_x, norm_mem, w_xq, w_xkv, w_xo, norm_ffn, w_gate_up, w_down, norm_final):
    b, s, _ = x.shape
    rows = s // GRID_W
    pos = jnp.arange(s, dtype=jnp.float32)
    row_pos = jnp.repeat(jnp.arange(rows, dtype=jnp.float32), GRID_W)
    col_pos = jnp.tile(jnp.arange(GRID_W, dtype=jnp.float32), rows)
    splits = []
    acc = 0
    for n in IN_SIZES[:-1]:
        acc += n
        splits.append(acc)

    for l in range(DEPTH):
        h = rms_norm(x, norm_mix[l])
        z = h @ w_in[l]
        qa, ka, va, cq, ckv, kr, qc, kc, vc = jnp.split(z, splits, axis=-1)

        qa = partial_rope(qa.reshape(b, s, A_HEADS, 2, A_QK), pos, A_ROT, ROPE_THETA)
        ka = partial_rope(ka.reshape(b, s, A_HEADS, 2, A_QK), pos, A_ROT, ROPE_THETA)
        lam_init = 0.8 - 0.6 * math.exp(-0.3 * l)
        lam = (jnp.exp(jnp.sum(lam_q1[l].astype(jnp.float32) * lam_k1[l].astype(jnp.float32)))
               - jnp.exp(jnp.sum(lam_q2[l].astype(jnp.float32) * lam_k2[l].astype(jnp.float32)))
               + lam_init)
        oa = diff_blocked(qa[..., 0, :], qa[..., 1, :], ka[..., 0, :], ka[..., 1, :],
                          va.reshape(b, s, A_HEADS, A_V), lam, A_QK ** -0.5)
        oa = rms_norm(oa, subln_a[l]) * (1.0 - lam_init)

        qb = (rms_norm(cq, mla_q_norm[l]) @ w_uq[l]).reshape(b, s, B_HEADS, B_NOPE + B_ROPE)
        qb = jnp.concatenate([qb[..., :B_NOPE], rope(qb[..., B_NOPE:], pos, MLA_THETA)], axis=-1)
        kvb = (rms_norm(ckv, mla_kv_norm[l]) @ w_ukv[l]).reshape(b, s, B_HEADS, B_NOPE + B_V)
        k_pe = rope(kr[:, :, None, :], pos, MLA_THETA)
        kb = jnp.concatenate([kvb[..., :B_NOPE],
                              jnp.broadcast_to(k_pe, (b, s, B_HEADS, B_ROPE))], axis=-1)
        ob = gqa_blocked(qb, kb, kvb[..., B_NOPE:], (B_NOPE + B_ROPE) ** -0.5)

        qc = rms_norm(qc.reshape(b, s, C_HEADS, C_HD), c_q_norm[l])
        kc = rms_norm(kc.reshape(b, s, C_KV_HEADS, C_HD), c_k_norm[l])
        half = C_HD // 2
        qc = jnp.concatenate([rope(qc[..., :half], row_pos, AXIAL_THETA),
                              rope(qc[..., half:], col_pos, AXIAL_THETA)], axis=-1)
        kc = jnp.concatenate([rope(kc[..., :half], row_pos, AXIAL_THETA),
                              rope(kc[..., half:], col_pos, AXIAL_THETA)], axis=-1)
        oc = gqa_blocked(qc, kc, vc.reshape(b, s, C_KV_HEADS, C_HD), C_HD ** -0.5)

        mix = jnp.concatenate([oa.reshape(b, s, A_WIDTH), ob.reshape(b, s, B_WIDTH),
                               oc.reshape(b, s, C_WIDTH)], axis=-1)
        x = x + mix @ w_o[l]

        hq = rms_norm(x, norm_x[l])
        m = rms_norm(mem, norm_mem[l])
        qx = (hq @ w_xq[l]).reshape(b, s, X_HEADS, X_HD)
        kvx = (m @ w_xkv[l]).reshape(b, mem.shape[1], 2, X_HEADS, X_HD)
        ox = gqa_blocked(qx, kvx[:, :, 0], kvx[:, :, 1], X_HD ** -0.5)
        x = x + ox.reshape(b, s, X_WIDTH) @ w_xo[l]

        hf = rms_norm(x, norm_ffn[l])
        gate, up = jnp.split(hf @ w_gate_up[l], 2, axis=-1)
        x = x + (jax.nn.silu(gate) * up) @ w_down[l]

    return rms_norm(x, norm_final)


def setup_inputs(seed: int = 0) -> dict:
    key = jax.random.key(seed)
    ks = iter(jax.random.split(key, 40))

    def nrm(shape, fan_in):
        return jax.random.normal(next(ks), shape, jnp.float32) * (fan_in ** -0.5)

    def gain(shape):
        return 1.0 + 0.05 * jax.random.normal(next(ks), shape, jnp.float32)

    def small(shape, scale):
        return scale * jax.random.normal(next(ks), shape, jnp.float32)

    L = DEPTH
    return {
        "x_prompt": jax.random.normal(next(ks), (BATCH, SEQ, D_MODEL), jnp.float32),
        "x_sample": jax.random.normal(next(ks), (DEC_BATCH, DEC_SEQ, D_MODEL), jnp.float32),
        "mem_prompt": jax.random.normal(next(ks), (BATCH, MEM_LEN, D_MODEL), jnp.float32),
        "mem_sample": jax.random.normal(next(ks), (DEC_BATCH, MEM_LEN, D_MODEL), jnp.float32),
        "norm_mix": gain((L, D_MODEL)),
        "w_in": nrm((L, D_MODEL, IN_COLS), D_MODEL),
        "lam_q1": small((L, A_QK), 0.1),
        "lam_k1": small((L, A_QK), 0.1),
        "lam_q2": small((L, A_QK), 0.1),
        "lam_k2": small((L, A_QK), 0.1),
        "subln_a": gain((L, A_V)),
        "mla_q_norm": gain((L, Q_LORA)),
        "w_uq": nrm((L, Q_LORA, B_HEADS * (B_NOPE + B_ROPE)), Q_LORA),
        "mla_kv_norm": gain((L, KV_LORA)),
        "w_ukv": nrm((L, KV_LORA, B_HEADS * (B_NOPE + B_V)), KV_LORA),
        "c_q_norm": gain((L, C_HD)),
        "c_k_norm": gain((L, C_HD)),
        "w_o": nrm((L, D_MIX, D_MODEL), D_MIX),
        "norm_x": gain((L, D_MODEL)),
        "norm_mem": gain((L, D_MODEL)),
        "w_xq": nrm((L, D_MODEL, X_WIDTH), D_MODEL),
        "w_xkv": nrm((L, D_MODEL, 2 * X_WIDTH), D_MODEL),
        "w_xo": nrm((L, X_WIDTH, D_MODEL), X_WIDTH),
        "norm_ffn": gain((L, D_MODEL)),
        "w_gate_up": nrm((L, D_MODEL, 2 * D_FF), D_MODEL),
        "w_down": nrm((L, D_FF, D_MODEL), D_FF),
        "norm_final": gain((D_MODEL,)),
    }


def reference(x_prompt, x_sample, mem_prompt, mem_sample, norm_mix, w_in, lam_q1, lam_k1,
              lam_q2, lam_k2, subln_a, mla_q_norm, w_uq, mla_kv_norm, w_ukv, c_q_norm,
              c_k_norm, w_o, norm_x, norm_mem, w_xq, w_xkv, w_xo, norm_ffn, w_gate_up,
              w_down, norm_final):
    y_prompt = encoder(x_prompt, mem_prompt, norm_mix, w_in, lam_q1, lam_k1, lam_q2, lam_k2,
                       subln_a, mla_q_norm, w_uq, mla_kv_norm, w_ukv, c_q_norm, c_k_norm, w_o,
                       norm_x, norm_mem, w_xq, w_xkv, w_xo, norm_ffn, w_gate_up, w_down,
                       norm_final)
    y_sample = encoder(x_sample, mem_sample, norm_mix, w_in, lam_q1, lam_k1, lam_q2, lam_k2,
                       subln_a, mla_q_norm, w_uq, mla_kv_norm, w_ukv, c_q_norm, c_k_norm, w_o,
                       norm_x, norm_mem, w_xq, w_xkv, w_xo, norm_ffn, w_gate_up, w_down,
                       norm_final)
    return (y_prompt, y_sample)
```

```python
import functools
import math

import jax
import jax.numpy as jnp
from jax import lax
from jax.experimental import pallas as pl
from jax.experimental.pallas import tpu as pltpu

F32 = jnp.float32
BF16 = jnp.bfloat16

D_MODEL = 1024
DEPTH = 2
GRID_W = 64
EPS = 1e-6
ROPE_THETA = 500000.0
MLA_THETA = 10000.0
AXIAL_THETA = 10000.0
A_HEADS, A_QK = 4, 32
A_V = 2 * A_QK
A_ROT = A_QK // 4
B_HEADS, B_NOPE, B_ROPE, B_V = 6, 64, 32, 64
Q_LORA, KV_LORA = 256, 128
C_HEADS, C_KV_HEADS, C_HD = 6, 2, 64
X_HEADS, X_HD = 4, 128
X_WIDTH = X_HEADS * X_HD
D_FF = -(-8 * D_MODEL // (3 * 256)) * 256
IN_SIZES = (A_HEADS * 2 * A_QK, A_HEADS * 2 * A_QK, A_HEADS * A_V, Q_LORA, KV_LORA, B_ROPE,
            C_HEADS * C_HD, C_KV_HEADS * C_HD, C_KV_HEADS * C_HD)
IN_COLS = sum(IN_SIZES)
_OFF = [0]
for _n in IN_SIZES:
    _OFF.append(_OFF[-1] + _n)
O_QA, O_KA, O_VA, O_CQ, O_CKV, O_KR, O_QC, O_KC, O_VC, _ = _OFF

LOG2E = math.log2(math.e)
HEAD_V = 64
LANES = 128
KD_A, KD_B, KD_C = 128, 256, 128

TS = 512
TQ = 512
TK = 512
VMEM_LIMIT = 56 * 1024 * 1024

NT_DIMS = (((1,), (1,)), ((), ()))

T_CA, T_SA, T_CB, T_SB, T_CR, T_SR, T_CC, T_SC, T_ROWS = 0, 8, 16, 32, 48, 64, 80, 96, 112


def _const_spec(shape):
    zeros = (0,) * len(shape)
    return pl.BlockSpec(shape, lambda *_: zeros, pipeline_mode=pl.Buffered(1))


def _rms_rows(x, g):
    return x * lax.rsqrt(jnp.mean(x * x, axis=-1, keepdims=True) + EPS) * g


def _rms_cols(x, g):
    return x * lax.rsqrt(jnp.mean(x * x, axis=0, keepdims=True) + EPS) * g


def _rot_half(x1, x2, c, s):
    return x1 * c - x2 * s, x1 * s + x2 * c


def _proj_kernel(x_ref, tab_ref, nmix_ref, winT_ref, gq_ref, wuqT_ref, gkv_ref, wukvT_ref,
                 gcq_ref, gck_ref,
                 qa_ref, ka_ref, va_ref, qb_ref, kb_ref, vb_ref, qc_ref, kc_ref, vc_ref):
    ts = x_ref.shape[0]
    nchunk = ts // TK
    h = _rms_rows(x_ref[...], nmix_ref[...]).astype(BF16)
    zT = lax.dot_general(winT_ref[...], h, NT_DIMS, preferred_element_type=F32)
    tab = tab_ref[...]
    cA, sA = tab[T_CA:T_CA + 8], tab[T_SA:T_SA + 8]
    cB, sB = tab[T_CB:T_CB + 16], tab[T_SB:T_SB + 16]
    cR, sR = tab[T_CR:T_CR + 16], tab[T_SR:T_SR + 16]
    cC, sC = tab[T_CC:T_CC + 16], tab[T_SC:T_SC + 16]

    def store_vt(ref, head, rows):
        for c in range(nchunk):
            ref[head, c] = rows[:, c * TK:(c + 1) * TK].astype(BF16)

    def rope_a(blk):
        top = blk[0:A_ROT]
        top = top * cA + pltpu.roll(top, A_ROT // 2, 0) * sA
        return jnp.concatenate([top, blk[A_ROT:A_QK]], axis=0)

    scale_a = (A_QK ** -0.5) * LOG2E
    qa_ref[...] = jnp.zeros(qa_ref.shape, BF16)
    ka_blocks = []
    for m in range(2 * A_HEADS):
        q = rope_a(zT[O_QA + A_QK * m:O_QA + A_QK * (m + 1)]) * scale_a
        r = m % 4
        qa_ref[m, A_QK * r:A_QK * (r + 1), :] = q.astype(BF16)
        ka_blocks.append(rope_a(zT[O_KA + A_QK * m:O_KA + A_QK * (m + 1)]))
    ka_ref[...] = jnp.concatenate(ka_blocks, axis=0).T.astype(BF16)
    for hh in range(A_HEADS):
        store_vt(va_ref, hh, zT[O_VA + A_V * hh:O_VA + A_V * (hh + 1)])

    scale_b = ((B_NOPE + B_ROPE) ** -0.5) * LOG2E
    cqn = _rms_cols(zT[O_CQ:O_CQ + Q_LORA], gq_ref[...]).astype(BF16)
    qbT = jnp.dot(wuqT_ref[...], cqn, preferred_element_type=F32)
    qb_ref[...] = jnp.zeros(qb_ref.shape, BF16)
    hd = B_NOPE + B_ROPE
    for hh in range(B_HEADS):
        blk = qbT[hd * hh:hd * (hh + 1)]
        e = hh % 2
        qb_ref[hh, B_NOPE * e:B_NOPE * (e + 1), :] = (blk[0:B_NOPE] * scale_b).astype(BF16)
        r1, r2 = _rot_half(blk[B_NOPE:B_NOPE + 16], blk[B_NOPE + 16:hd], cB, sB)
        qb_ref[hh, 2 * B_NOPE:2 * B_NOPE + B_ROPE, :] = (
            jnp.concatenate([r1, r2], axis=0) * scale_b).astype(BF16)
    ckvn = _rms_cols(zT[O_CKV:O_CKV + KV_LORA], gkv_ref[...]).astype(BF16)
    kvbT = jnp.dot(wukvT_ref[...], ckvn, preferred_element_type=F32)
    kr = zT[O_KR:O_KR + B_ROPE]
    p1, p2 = _rot_half(kr[0:16], kr[16:32], cB, sB)
    peT = jnp.concatenate([p1, p2, jnp.zeros((LANES - B_ROPE, ts), F32)], axis=0)
    pe = peT.T.astype(BF16)
    for p in range(B_HEADS // 2):
        kb_ref[p, :, 0:LANES] = kvbT[LANES * p:LANES * (p + 1)].T.astype(BF16)
        kb_ref[p, :, LANES:2 * LANES] = pe
    for hh in range(B_HEADS):
        store_vt(vb_ref, hh, kvbT[B_HEADS * B_NOPE + B_V * hh:B_HEADS * B_NOPE + B_V * (hh + 1)])

    def norm_rope_c(blk, g):
        y = _rms_cols(blk, g)
        a1, a2 = _rot_half(y[0:16], y[16:32], cR, sR)
        b1, b2 = _rot_half(y[32:48], y[48:64], cC, sC)
        return jnp.concatenate([a1, a2, b1, b2], axis=0)

    scale_c = (C_HD ** -0.5) * LOG2E
    qc_ref[...] = jnp.zeros(qc_ref.shape, BF16)
    rep = C_HEADS // C_KV_HEADS
    for j in range(C_HEADS):
        g = j // rep
        q = norm_rope_c(zT[O_QC + C_HD * j:O_QC + C_HD * (j + 1)], gcq_ref[...]) * scale_c
        qc_ref[j, C_HD * g:C_HD * (g + 1), :] = q.astype(BF16)
    kc = [norm_rope_c(zT[O_KC + C_HD * g:O_KC + C_HD * (g + 1)], gck_ref[...]) for g in range(C_KV_HEADS)]
    kc_ref[...] = jnp.concatenate(kc, axis=0).T.astype(BF16)
    for g in range(C_KV_HEADS):
        store_vt(vc_ref, g, zT[O_VC + C_HD * g:O_VC + C_HD * (g + 1)])


def _proj(x, tabT, nmix, winT, gq, wuqT, gkv, wukvT, gcq, gck):
    b, s, _ = x.shape
    nk = s // TK
    cpt = TS // TK
    out_shape = (
        jax.ShapeDtypeStruct((b, 2 * A_HEADS, KD_A, s), BF16),
        jax.ShapeDtypeStruct((b, s, 2 * A_HEADS * A_QK), BF16),
        jax.ShapeDtypeStruct((b, A_HEADS, nk, HEAD_V, TK), BF16),
        jax.ShapeDtypeStruct((b, B_HEADS, KD_B, s), BF16),
        jax.ShapeDtypeStruct((b, B_HEADS // 2, s, KD_B), BF16),
        jax.ShapeDtypeStruct((b, B_HEADS, nk, HEAD_V, TK), BF16),
        jax.ShapeDtypeStruct((b, C_HEADS, KD_C, s), BF16),
        jax.ShapeDtypeStruct((b, s, C_KV_HEADS * C_HD), BF16),
        jax.ShapeDtypeStruct((b, C_KV_HEADS, nk, HEAD_V, TK), BF16),
    )
    qt_spec = lambda n, kd: pl.BlockSpec((None, n, kd, TS), lambda bi, i: (bi, 0, 0, i))
    vt_spec = lambda n: pl.BlockSpec((None, n, cpt, HEAD_V, TK), lambda bi, i: (bi, 0, i, 0, 0))
    out_specs = (
        qt_spec(2 * A_HEADS, KD_A),
        pl.BlockSpec((None, TS, 2 * A_HEADS * A_QK), lambda bi, i: (bi, i, 0)),
        vt_spec(A_HEADS),
        qt_spec(B_HEADS, KD_B),
        pl.BlockSpec((None, B_HEADS // 2, TS, KD_B), lambda bi, i: (bi, 0, i, 0)),
        vt_spec(B_HEADS),
        qt_spec(C_HEADS, KD_C),
        pl.BlockSpec((None, TS, C_KV_HEADS * C_HD), lambda bi, i: (bi, i, 0)),
        vt_spec(C_KV_HEADS),
    )
    in_specs = [
        pl.BlockSpec((None, TS, D_MODEL), lambda bi, i: (bi, i, 0)),
        pl.BlockSpec((T_ROWS, TS), lambda bi, i: (0, i)),
        _const_spec(nmix.shape), _const_spec(winT.shape), _const_spec(gq.shape), _const_spec(wuqT.shape),
        _const_spec(gkv.shape), _const_spec(wukvT.shape), _const_spec(gcq.shape), _const_spec(gck.shape),
    ]
    return pl.pallas_call(
        _proj_kernel, grid=(b, s // TS), in_specs=in_specs, out_specs=out_specs, out_shape=out_shape,
        compiler_params=pltpu.CompilerParams(dimension_semantics=("arbitrary", "arbitrary"),
                                             vmem_limit_bytes=VMEM_LIMIT),
        name="proj",
    )(x, tabT, nmix, winT, gq, wuqT, gkv, wukvT, gcq, gck)


def _flash_kernel(*refs, mode, lam_init):
    if mode == "A":
        qt_ref, k_ref, vt_ref, lam_ref, sub_ref, o_ref, m_ref, l_ref, acc_ref = refs
    else:
        qt_ref, k_ref, vt_ref, o_ref, m_ref, l_ref, acc_ref = refs
    nmaps = qt_ref.shape[0]
    nk = k_ref.shape[0] // TK
    if mode == "A":
        vidx = (0, 0, 1, 1)
    elif mode == "B":
        vidx = (0, 1)
    else:
        pair = pl.program_id(1)
        vidx = (pair // 2, (pair + 1) // 2)

    m_ref[...] = jnp.full(m_ref.shape, -jnp.inf, F32)
    l_ref[...] = jnp.zeros(l_ref.shape, F32)
    acc_ref[...] = jnp.zeros(acc_ref.shape, F32)

    def body(j, carry):
        k_blk = k_ref[pl.ds(pl.multiple_of(j * TK, TK), TK), :]
        for mi in range(nmaps):
            s = jnp.dot(k_blk, qt_ref[mi], preferred_element_type=F32)
            m_old = m_ref[mi]
            m_new = jnp.maximum(m_old, jnp.max(s, axis=0, keepdims=True))
            p = jnp.exp2(s - m_new)
            alpha = jnp.exp2(m_old - m_new)
            l_ref[mi] = alpha * l_ref[mi] + jnp.sum(p, axis=0, keepdims=True)
            pv = jnp.dot(vt_ref[vidx[mi], j], p.astype(BF16), preferred_element_type=F32)
            acc_ref[mi] = acc_ref[mi] * alpha + pv
            m_ref[mi] = m_new
        return carry

    lax.fori_loop(0, nk, body, 0)

    if mode == "A":
        lp = lam_ref[...]
        lam = (jnp.exp(jnp.sum(lp[0:1] * lp[1:2], axis=-1, keepdims=True))
               - jnp.exp(jnp.sum(lp[2:3] * lp[3:4], axis=-1, keepdims=True)) + lam_init)
        outs = []
        for e in range(2):
            o = acc_ref[2 * e] / l_ref[2 * e] - lam * (acc_ref[2 * e + 1] / l_ref[2 * e + 1])
            outs.append(_rms_cols(o, sub_ref[...]) * (1.0 - lam_init))
    else:
        outs = [acc_ref[e] / l_ref[e] for e in range(2)]
    o_ref[...] = jnp.concatenate(outs, axis=0).T.astype(BF16)


def _flash(mode, qt, k, vt, *, lam_init=0.0, lam_params=None, subln=None):
    b, nmaps_total, kd, s = qt.shape
    npairs = {"A": A_HEADS // 2, "B": B_HEADS // 2, "C": C_HEADS // 2}[mode]
    nmaps = nmaps_total // npairs
    nk = s // TK
    qt_spec = pl.BlockSpec((None, nmaps, kd, TQ), lambda bi, p, i: (bi, p, 0, i))
    if mode == "A":
        k_spec = pl.BlockSpec((None, s, kd), lambda bi, p, i: (bi, 0, p))
        vt_spec = pl.BlockSpec((None, 2, nk, HEAD_V, TK), lambda bi, p, i: (bi, p, 0, 0, 0))
    elif mode == "B":
        k_spec = pl.BlockSpec((None, None, s, kd), lambda bi, p, i: (bi, p, 0, 0))
        vt_spec = pl.BlockSpec((None, 2, nk, HEAD_V, TK), lambda bi, p, i: (bi, p, 0, 0, 0))
    else:
        k_spec = pl.BlockSpec((None, s, kd), lambda bi, p, i: (bi, 0, 0))
        vt_spec = pl.BlockSpec((None, C_KV_HEADS, nk, HEAD_V, TK), lambda bi, p, i: (bi, 0, 0, 0, 0))
    in_specs = [qt_spec, k_spec, vt_spec]
    args = [qt, k, vt]
    if mode == "A":
        in_specs += [_const_spec(lam_params.shape), _const_spec(subln.shape)]
        args += [lam_params, subln]
    return pl.pallas_call(
        functools.partial(_flash_kernel, mode=mode, lam_init=lam_init),
        grid=(b, npairs, s // TQ),
        in_specs=in_specs,
        out_specs=pl.BlockSpec((None, TQ, LANES), lambda bi, p, i: (bi, i, p)),
        out_shape=jax.ShapeDtypeStruct((b, s, npairs * LANES), BF16),
        scratch_shapes=[pltpu.VMEM((nmaps, 1, TQ), F32), pltpu.VMEM((nmaps, 1, TQ), F32),
                        pltpu.VMEM((nmaps, HEAD_V, TQ), F32)],
        compiler_params=pltpu.CompilerParams(dimension_semantics=("arbitrary",) * 3,
                                             vmem_limit_bytes=VMEM_LIMIT),
        name="flash_" + mode,
    )(*args)


def _memkv_kernel(mem_ref, g_ref, w_ref, kv_ref):
    m = _rms_rows(mem_ref[...], g_ref[...]).astype(BF16)
    kv_ref[...] = jnp.dot(m, w_ref[...], preferred_element_type=F32).astype(BF16)


def _memkv(mem, g, w):
    b, t, _ = mem.shape
    return pl.pallas_call(
        _memkv_kernel, grid=(b,),
        in_specs=[pl.BlockSpec((None, t, D_MODEL), lambda bi: (bi, 0, 0)), _const_spec(g.shape), _const_spec(w.shape)],
        out_specs=pl.BlockSpec((None, t, 2 * X_WIDTH), lambda bi: (bi, 0, 0)),
        out_shape=jax.ShapeDtypeStruct((b, t, 2 * X_WIDTH), BF16),
        compiler_params=pltpu.CompilerParams(dimension_semantics=("arbitrary",), vmem_limit_bytes=VMEM_LIMIT),
        name="memkv",
    )(mem, g, w)


def _post_kernel(x_ref, ma_ref, mb_ref, mc_ref, kv_ref, woa_ref, wob_ref, woc_ref, nx_ref, wxq_ref, wxo_ref, o_ref):
    x = x_ref[...]
    x = x + (jnp.dot(ma_ref[...], woa_ref[...], preferred_element_type=F32)
             + jnp.dot(mb_ref[...], wob_ref[...], preferred_element_type=F32)
             + jnp.dot(mc_ref[...], woc_ref[...], preferred_element_type=F32))
    hq = _rms_rows(x, nx_ref[...]).astype(BF16)
    qx = jnp.dot(hq, wxq_ref[...], preferred_element_type=F32).astype(BF16)
    kv = kv_ref[...]
    heads = []
    for hh in range(X_HEADS):
        q = qx[:, X_HD * hh:X_HD * (hh + 1)]
        kx = kv[:, X_HD * hh:X_HD * (hh + 1)]
        vx = kv[:, X_WIDTH + X_HD * hh:X_WIDTH + X_HD * (hh + 1)]
        sc = lax.dot_general(q, kx, NT_DIMS, preferred_element_type=F32) * (X_HD ** -0.5)
        p = jnp.exp(sc - jnp.max(sc, axis=-1, keepdims=True))
        l = jnp.sum(p, axis=-1, keepdims=True)
        heads.append(jnp.dot(p.astype(BF16), vx, preferred_element_type=F32) / l)
    ox = jnp.concatenate(heads, axis=-1).astype(BF16)
    o_ref[...] = x + jnp.dot(ox, wxo_ref[...], preferred_element_type=F32)


def _post(x, ma, mb, mc, kv, woa, wob, woc, nx, wxq, wxo):
    b, s, _ = x.shape
    t = kv.shape[1]
    tile = lambda w: pl.BlockSpec((None, TS, w), lambda bi, i: (bi, i, 0))
    return pl.pallas_call(
        _post_kernel, grid=(b, s // TS),
        in_specs=[tile(D_MODEL), tile(ma.shape[-1]), tile(mb.shape[-1]), tile(mc.shape[-1]),
                  pl.BlockSpec((None, t, 2 * X_WIDTH), lambda bi, i: (bi, 0, 0)),
                  _const_spec(woa.shape), _const_spec(wob.shape), _const_spec(woc.shape),
                  _const_spec(nx.shape), _const_spec(wxq.shape), _const_spec(wxo.shape)],
        out_specs=tile(D_MODEL),
        out_shape=jax.ShapeDtypeStruct(x.shape, F32),
        compiler_params=pltpu.CompilerParams(dimension_semantics=("arbitrary", "arbitrary"),
                                             vmem_limit_bytes=VMEM_LIMIT),
        name="post",
    )(x, ma, mb, mc, kv, woa, wob, woc, nx, wxq, wxo)


def _ffn_kernel(x_ref, nf_ref, wgu_ref, wd_ref, nfin_ref, o_ref, *, final):
    x = x_ref[...]
    hf = _rms_rows(x, nf_ref[...]).astype(BF16)
    gu = jnp.dot(hf, wgu_ref[...], preferred_element_type=F32)
    gate, up = gu[:, :D_FF], gu[:, D_FF:]
    act = (gate / (1.0 + jnp.exp(-gate)) * up).astype(BF16)
    y = x + jnp.dot(act, wd_ref[...], preferred_element_type=F32)
    if final:
        y = _rms_rows(y, nfin_ref[...])
    o_ref[...] = y


def _ffn(x, nf, wgu, wd, nfin, final):
    b, s, _ = x.shape
    tile = pl.BlockSpec((None, TS, D_MODEL), lambda bi, i: (bi, i, 0))
    return pl.pallas_call(
        functools.partial(_ffn_kernel, final=final), grid=(b, s // TS),
        in_specs=[tile, _const_spec(nf.shape), _const_spec(wgu.shape), _const_spec(wd.shape), _const_spec(nfin.shape)],
        out_specs=tile,
        out_shape=jax.ShapeDtypeStruct(x.shape, F32),
        compiler_params=pltpu.CompilerParams(dimension_semantics=("arbitrary", "arbitrary"),
                                             vmem_limit_bytes=VMEM_LIMIT),
        name="ffn",
    )(x, nf, wgu, wd, nfin)


def _rope_table(pos, d, theta):
    half = d // 2
    inv = theta ** (-jnp.arange(half, dtype=F32) * (2.0 / d))
    ang = pos[:, None] * inv[None, :]
    return jnp.cos(ang).T, jnp.sin(ang).T


def _tables(s):
    pos = jnp.arange(s, dtype=F32)
    rows = s // GRID_W
    row_pos = jnp.repeat(jnp.arange(rows, dtype=F32), GRID_W)
    col_pos = jnp.tile(jnp.arange(GRID_W, dtype=F32), rows)
    ca, sa = _rope_table(pos, A_ROT, ROPE_THETA)
    cb, sb = _rope_table(pos, B_ROPE, MLA_THETA)
    cr, sr = _rope_table(row_pos, C_HD // 2, AXIAL_THETA)
    cc, sc = _rope_table(col_pos, C_HD // 2, AXIAL_THETA)
    return jnp.concatenate([ca, ca, -sa, sa, cb, sb, cr, sr, cc, sc], axis=0)


def _encoder(x, mem, tabT, wts):
    for l in range(DEPTH):
        w = wts[l]
        lam_init = 0.8 - 0.6 * math.exp(-0.3 * l)
        qa, ka, va, qb, kb, vb, qc, kc, vc = _proj(x, tabT, w["nmix"], w["winT"], w["gq"], w["wuqT"],
                                                   w["gkv"], w["wukvT"], w["gcq"], w["gck"])
        ma = _flash("A", qa, ka, va, lam_init=lam_init, lam_params=w["lam"], subln=w["subln"])
        mb = _flash("B", qb, kb, vb)
        mc = _flash("C", qc, kc, vc)
        kv = _memkv(mem, w["nmem"], w["wxkv"])
        x = _post(x, ma, mb, mc, kv, w["woa"], w["wob"], w["woc"], w["nx"], w["wxq"], w["wxo"])
        x = _ffn(x, w["nffn"], w["wgu"], w["wd"], w["nfinal"], final=(l == DEPTH - 1))
    return x


def kernel(x_prompt, x_sample, mem_prompt, mem_sample, norm_mix, w_in, lam_q1, lam_k1, lam_q2, lam_k2, subln_a, mla_q_norm, w_uq, mla_kv_norm, w_ukv, c_q_norm, c_k_norm, w_o, norm_x, norm_mem, w_xq, w_xkv, w_xo, norm_ffn, w_gate_up, w_down, norm_final):
    s = x_prompt.shape[1]
    assert x_sample.shape[1] == s and s % TS == 0 and s % TQ == 0 and s % TK == 0 and TS % TK == 0
    tabT = _tables(s)
    row = lambda v: v.reshape(1, -1).astype(F32)
    col = lambda v: v.reshape(-1, 1).astype(F32)
    a_w, b_w = A_HEADS * A_V, B_HEADS * B_V
    wts = []
    for l in range(DEPTH):
        ukv = w_ukv[l].reshape(KV_LORA, B_HEADS, B_NOPE + B_V)
        ukv = jnp.concatenate([ukv[:, :, :B_NOPE].reshape(KV_LORA, -1), ukv[:, :, B_NOPE:].reshape(KV_LORA, -1)], axis=1)
        wts.append(dict(
            nmix=row(norm_mix[l]), winT=w_in[l].T.astype(BF16),
            gq=col(mla_q_norm[l]), wuqT=w_uq[l].T.astype(BF16),
            gkv=col(mla_kv_norm[l]), wukvT=ukv.T.astype(BF16),
            gcq=col(c_q_norm[l]), gck=col(c_k_norm[l]),
            lam=jnp.stack([lam_q1[l], lam_k1[l], lam_q2[l], lam_k2[l]]).astype(F32), subln=col(subln_a[l]),
            woa=w_o[l, :a_w].astype(BF16), wob=w_o[l, a_w:a_w + b_w].astype(BF16), woc=w_o[l, a_w + b_w:].astype(BF16),
            nx=row(norm_x[l]), nmem=row(norm_mem[l]),
            wxq=w_xq[l].astype(BF16), wxkv=w_xkv[l].astype(BF16), wxo=w_xo[l].astype(BF16),
            nffn=row(norm_ffn[l]), wgu=w_gate_up[l].astype(BF16), wd=w_down[l].astype(BF16),
            nfinal=row(norm_final),
        ))
    y_prompt = _encoder(x_prompt, mem_prompt, tabT, wts)
    y_sample = _encoder(x_sample, mem_sample, tabT, wts)
    return (y_prompt, y_sample)
```

```python
import functools
import math

import jax
import jax.numpy as jnp
from jax import lax
from jax.experimental import pallas as pl
from jax.experimental.pallas import tpu as pltpu

F32 = jnp.float32
BF16 = jnp.bfloat16

D_MODEL = 1024
DEPTH = 2
GRID_W = 64
EPS = 1e-6
ROPE_THETA = 500000.0
MLA_THETA = 10000.0
AXIAL_THETA = 10000.0
A_HEADS, A_QK = 4, 32
A_V = 2 * A_QK
A_ROT = A_QK // 4
B_HEADS, B_NOPE, B_ROPE, B_V = 6, 64, 32, 64
Q_LORA, KV_LORA = 256, 128
C_HEADS, C_KV_HEADS, C_HD = 6, 2, 64
X_HEADS, X_HD = 4, 128
X_WIDTH = X_HEADS * X_HD
D_FF = -(-8 * D_MODEL // (3 * 256)) * 256
IN_SIZES = (A_HEADS * 2 * A_QK, A_HEADS * 2 * A_QK, A_HEADS * A_V, Q_LORA, KV_LORA, B_ROPE,
            C_HEADS * C_HD, C_KV_HEADS * C_HD, C_KV_HEADS * C_HD)
IN_COLS = sum(IN_SIZES)
_OFF = [0]
for _n in IN_SIZES:
    _OFF.append(_OFF[-1] + _n)
O_QA, O_KA, O_VA, O_CQ, O_CKV, O_KR, O_QC, O_KC, O_VC, _ = _OFF

LOG2E = math.log2(math.e)
HEAD_V = 64
LANES = 128
KD_A, KD_B, KD_C = 128, 256, 128

TS = 512
TQ = 512
TK = 512
TILES_PER_ITER = 4
ROWS = 64
VMEM_LIMIT = 56 * 1024 * 1024

NT_DIMS = (((1,), (1,)), ((), ()))

T_CA, T_SA, T_CB, T_SB, T_CR, T_SR, T_CC, T_SC, T_ROWS = 0, 8, 16, 32, 48, 64, 80, 96, 112


def _const_spec(shape):
    zeros = (0,) * len(shape)
    return pl.BlockSpec(shape, lambda *_: zeros, pipeline_mode=pl.Buffered(1))


def _rms_rows(x, g):
    return x * lax.rsqrt(jnp.mean(x * x, axis=-1, keepdims=True) + EPS) * g


def _rms_cols(x, g):
    return x * lax.rsqrt(jnp.mean(x * x, axis=0, keepdims=True) + EPS) * g


def _rot_half(x1, x2, c, s):
    return x1 * c - x2 * s, x1 * s + x2 * c


def _proj_kernel(x_ref, tab_ref, nmix_ref, winT_ref, gq_ref, wuqT_ref, gkv_ref, wukvT_ref,
                 gcq_ref, gck_ref,
                 qa_ref, ka_ref, va_ref, qb_ref, kb_ref, vb_ref, qc_ref, kc_ref, vc_ref):
    ts = x_ref.shape[0]
    nchunk = ts // TK
    h = _rms_rows(x_ref[...], nmix_ref[...]).astype(BF16)
    zT = lax.dot_general(winT_ref[...], h, NT_DIMS, preferred_element_type=F32)
    tab = tab_ref[...]
    cA, sA = tab[T_CA:T_CA + 8], tab[T_SA:T_SA + 8]
    cB, sB = tab[T_CB:T_CB + 16], tab[T_SB:T_SB + 16]
    cR, sR = tab[T_CR:T_CR + 16], tab[T_SR:T_SR + 16]
    cC, sC = tab[T_CC:T_CC + 16], tab[T_SC:T_SC + 16]

    def store_vt(ref, head, rows):
        for c in range(nchunk):
            ref[head, c] = rows[:, c * TK:(c + 1) * TK].astype(BF16)

    def rope_a(blk):
        top = blk[0:A_ROT]
        top = top * cA + pltpu.roll(top, A_ROT // 2, 0) * sA
        return jnp.concatenate([top, blk[A_ROT:A_QK]], axis=0)

    scale_a = (A_QK ** -0.5) * LOG2E
    qa_ref[...] = jnp.zeros(qa_ref.shape, BF16)
    ka_blocks = []
    for m in range(2 * A_HEADS):
        q = rope_a(zT[O_QA + A_QK * m:O_QA + A_QK * (m + 1)]) * scale_a
        r = m % 4
        qa_ref[m, A_QK * r:A_QK * (r + 1), :] = q.astype(BF16)
        ka_blocks.append(rope_a(zT[O_KA + A_QK * m:O_KA + A_QK * (m + 1)]))
    ka_ref[...] = jnp.concatenate(ka_blocks, axis=0).T.astype(BF16)
    for hh in range(A_HEADS):
        store_vt(va_ref, hh, zT[O_VA + A_V * hh:O_VA + A_V * (hh + 1)])

    scale_b = ((B_NOPE + B_ROPE) ** -0.5) * LOG2E
    cqn = _rms_cols(zT[O_CQ:O_CQ + Q_LORA], gq_ref[...]).astype(BF16)
    qbT = jnp.dot(wuqT_ref[...], cqn, preferred_element_type=F32)
    qb_ref[...] = jnp.zeros(qb_ref.shape, BF16)
    hd = B_NOPE + B_ROPE
    for hh in range(B_HEADS):
        blk = qbT[hd * hh:hd * (hh + 1)]
        e = hh % 2
        qb_ref[hh, B_NOPE * e:B_NOPE * (e + 1), :] = (blk[0:B_NOPE] * scale_b).astype(BF16)
        r1, r2 = _rot_half(blk[B_NOPE:B_NOPE + 16], blk[B_NOPE + 16:hd], cB, sB)
        qb_ref[hh, 2 * B_NOPE:2 * B_NOPE + B_ROPE, :] = (
            jnp.concatenate([r1, r2], axis=0) * scale_b).astype(BF16)
    ckvn = _rms_cols(zT[O_CKV:O_CKV + KV_LORA], gkv_ref[...]).astype(BF16)
    kvbT = jnp.dot(wukvT_ref[...], ckvn, preferred_element_type=F32)
    kr = zT[O_KR:O_KR + B_ROPE]
    p1, p2 = _rot_half(kr[0:16], kr[16:32], cB, sB)
    peT = jnp.concatenate([p1, p2, jnp.zeros((LANES - B_ROPE, ts), F32)], axis=0)
    pe = peT.T.astype(BF16)
    for p in range(B_HEADS // 2):
        kb_ref[p, :, 0:LANES] = kvbT[LANES * p:LANES * (p + 1)].T.astype(BF16)
        kb_ref[p, :, LANES:2 * LANES] = pe
    for hh in range(B_HEADS):
        store_vt(vb_ref, hh, kvbT[B_HEADS * B_NOPE + B_V * hh:B_HEADS * B_NOPE + B_V * (hh + 1)])

    def norm_rope_c(blk, g):
        y = _rms_cols(blk, g)
        a1, a2 = _rot_half(y[0:16], y[16:32], cR, sR)
        b1, b2 = _rot_half(y[32:48], y[48:64], cC, sC)
        return jnp.concatenate([a1, a2, b1, b2], axis=0)

    scale_c = (C_HD ** -0.5) * LOG2E
    qc_ref[...] = jnp.zeros(qc_ref.shape, BF16)
    rep = C_HEADS // C_KV_HEADS
    for j in range(C_HEADS):
        g = j // rep
        q = norm_rope_c(zT[O_QC + C_HD * j:O_QC + C_HD * (j + 1)], gcq_ref[...]) * scale_c
        qc_ref[j, C_HD * g:C_HD * (g + 1), :] = q.astype(BF16)
    kc = [norm_rope_c(zT[O_KC + C_HD * g:O_KC + C_HD * (g + 1)], gck_ref[...]) for g in range(C_KV_HEADS)]
    kc_ref[...] = jnp.concatenate(kc, axis=0).T.astype(BF16)
    for g in range(C_KV_HEADS):
        store_vt(vc_ref, g, zT[O_VC + C_HD * g:O_VC + C_HD * (g + 1)])


def _proj(x, tabT, nmix, winT, gq, wuqT, gkv, wukvT, gcq, gck):
    b, s, _ = x.shape
    nk = s // TK
    cpt = TS // TK
    out_shape = (
        jax.ShapeDtypeStruct((b, 2 * A_HEADS, KD_A, s), BF16),
        jax.ShapeDtypeStruct((b, s, 2 * A_HEADS * A_QK), BF16),
        jax.ShapeDtypeStruct((b, A_HEADS, nk, HEAD_V, TK), BF16),
        jax.ShapeDtypeStruct((b, B_HEADS, KD_B, s), BF16),
        jax.ShapeDtypeStruct((b, B_HEADS // 2, s, KD_B), BF16),
        jax.ShapeDtypeStruct((b, B_HEADS, nk, HEAD_V, TK), BF16),
        jax.ShapeDtypeStruct((b, C_HEADS, KD_C, s), BF16),
        jax.ShapeDtypeStruct((b, s, C_KV_HEADS * C_HD), BF16),
        jax.ShapeDtypeStruct((b, C_KV_HEADS, nk, HEAD_V, TK), BF16),
    )
    qt_spec = lambda n, kd: pl.BlockSpec((None, n, kd, TS), lambda bi, i: (bi, 0, 0, i))
    vt_spec = lambda n: pl.BlockSpec((None, n, cpt, HEAD_V, TK), lambda bi, i: (bi, 0, i, 0, 0))
    out_specs = (
        qt_spec(2 * A_HEADS, KD_A),
        pl.BlockSpec((None, TS, 2 * A_HEADS * A_QK), lambda bi, i: (bi, i, 0)),
        vt_spec(A_HEADS),
        qt_spec(B_HEADS, KD_B),
        pl.BlockSpec((None, B_HEADS // 2, TS, KD_B), lambda bi, i: (bi, 0, i, 0)),
        vt_spec(B_HEADS),
        qt_spec(C_HEADS, KD_C),
        pl.BlockSpec((None, TS, C_KV_HEADS * C_HD), lambda bi, i: (bi, i, 0)),
        vt_spec(C_KV_HEADS),
    )
    in_specs = [
        pl.BlockSpec((None, TS, D_MODEL), lambda bi, i: (bi, i, 0)),
        pl.BlockSpec((T_ROWS, TS), lambda bi, i: (0, i)),
        _const_spec(nmix.shape), _const_spec(winT.shape), _const_spec(gq.shape), _const_spec(wuqT.shape),
        _const_spec(gkv.shape), _const_spec(wukvT.shape), _const_spec(gcq.shape), _const_spec(gck.shape),
    ]
    return pl.pallas_call(
        _proj_kernel, grid=(b, s // TS), in_specs=in_specs, out_specs=out_specs, out_shape=out_shape,
        compiler_params=pltpu.CompilerParams(dimension_semantics=("arbitrary", "arbitrary"),
                                             vmem_limit_bytes=VMEM_LIMIT),
        name="proj",
    )(x, tabT, nmix, winT, gq, wuqT, gkv, wukvT, gcq, gck)


def _flash_kernel(*refs, mode, lam_init, chunks_per_iter):
    if mode == "A":
        qt_ref, k_ref, vt_ref, lam_ref, sub_ref, o_ref, m_ref, l_ref, acc_ref, s_scr, x_scr, p_scr, a_scr = refs
    else:
        qt_ref, k_ref, vt_ref, o_ref, m_ref, l_ref, acc_ref, s_scr, x_scr, p_scr, a_scr = refs
    nmaps = qt_ref.shape[0]
    nk = k_ref.shape[0] // TK
    cpi = chunks_per_iter
    if mode == "A":
        vidx = (0, 0, 1, 1)
    elif mode == "B":
        vidx = (0, 1)
    else:
        pair = pl.program_id(1)
        vidx = (pair // 2, (pair + 1) // 2)

    def scores(chunk, mi, slot):
        k_blk = k_ref[pl.ds(pl.multiple_of(chunk * TK, TK), TK), :]
        s = jnp.dot(k_blk, qt_ref[mi], preferred_element_type=F32)
        s_scr[slot] = s
        x_scr[slot] = jnp.max(s, axis=0, keepdims=True)

    def softmax_update(mi, slot):
        m_old = m_ref[mi]
        m_new = jnp.maximum(m_old, x_scr[slot])
        alpha = jnp.exp2(m_old - m_new)
        psum = jnp.zeros((8, TQ), F32)
        for r in range(TK // ROWS):
            p = jnp.exp2(s_scr[slot, r * ROWS:(r + 1) * ROWS, :] - m_new)
            psum = psum + jnp.sum(p.reshape(ROWS // 8, 8, TQ), axis=0)
            p_scr[slot, r * ROWS:(r + 1) * ROWS, :] = p.astype(BF16)
        l_ref[mi] = alpha * l_ref[mi] + jnp.sum(psum, axis=0, keepdims=True)
        m_ref[mi] = m_new
        a_scr[slot] = alpha

    def accumulate(chunk, mi, slot):
        pv = jnp.dot(vt_ref[vidx[mi], chunk], p_scr[slot], preferred_element_type=F32)
        acc_ref[mi] = acc_ref[mi] * a_scr[slot] + pv

    m_ref[...] = jnp.full(m_ref.shape, -jnp.inf, F32)
    l_ref[...] = jnp.zeros(l_ref.shape, F32)
    acc_ref[...] = jnp.zeros(acc_ref.shape, F32)
    scores(0, 0, 0)
    p_scr[1] = jnp.zeros(p_scr.shape[1:], BF16)
    a_scr[1] = jnp.ones(a_scr.shape[1:], F32)

    def body(it, carry):
        for c in range(cpi):
            chunk = it * cpi + c
            for mi in range(nmaps):
                slot = (c * nmaps + mi) % 2
                if mi + 1 < nmaps:
                    scores(chunk, mi + 1, 1 - slot)
                elif c + 1 < cpi:
                    scores(chunk + 1, 0, 1 - slot)
                else:
                    scores(jnp.minimum(chunk + 1, nk - 1), 0, 1 - slot)
                if mi > 0:
                    accumulate(chunk, mi - 1, 1 - slot)
                elif c > 0:
                    accumulate(chunk - 1, nmaps - 1, 1 - slot)
                else:
                    accumulate(jnp.maximum(chunk - 1, 0), nmaps - 1, 1 - slot)
                softmax_update(mi, slot)
        return carry

    lax.fori_loop(0, nk // cpi, body, 0)
    accumulate(nk - 1, nmaps - 1, 1)

    if mode == "A":
        lp = lam_ref[...]
        lam = (jnp.exp(jnp.sum(lp[0:1] * lp[1:2], axis=-1, keepdims=True))
               - jnp.exp(jnp.sum(lp[2:3] * lp[3:4], axis=-1, keepdims=True)) + lam_init)
        outs = []
        for e in range(2):
            o = acc_ref[2 * e] / l_ref[2 * e] - lam * (acc_ref[2 * e + 1] / l_ref[2 * e + 1])
            outs.append(_rms_cols(o, sub_ref[...]) * (1.0 - lam_init))
    else:
        outs = [acc_ref[e] / l_ref[e] for e in range(2)]
    o_ref[...] = jnp.concatenate(outs, axis=0).T.astype(BF16)


def _flash(mode, qt, k, vt, *, lam_init=0.0, lam_params=None, subln=None):
    b, nmaps_total, kd, s = qt.shape
    npairs = {"A": A_HEADS // 2, "B": B_HEADS // 2, "C": C_HEADS // 2}[mode]
    nmaps = nmaps_total // npairs
    nk = s // TK
    qt_spec = pl.BlockSpec((None, nmaps, kd, TQ), lambda bi, p, i: (bi, p, 0, i))
    if mode == "A":
        k_spec = pl.BlockSpec((None, s, kd), lambda bi, p, i: (bi, 0, p))
        vt_spec = pl.BlockSpec((None, 2, nk, HEAD_V, TK), lambda bi, p, i: (bi, p, 0, 0, 0))
    elif mode == "B":
        k_spec = pl.BlockSpec((None, None, s, kd), lambda bi, p, i: (bi, p, 0, 0))
        vt_spec = pl.BlockSpec((None, 2, nk, HEAD_V, TK), lambda bi, p, i: (bi, p, 0, 0, 0))
    else:
        k_spec = pl.BlockSpec((None, s, kd), lambda bi, p, i: (bi, 0, 0))
        vt_spec = pl.BlockSpec((None, C_KV_HEADS, nk, HEAD_V, TK), lambda bi, p, i: (bi, 0, 0, 0, 0))
    in_specs = [qt_spec, k_spec, vt_spec]
    args = [qt, k, vt]
    if mode == "A":
        in_specs += [_const_spec(lam_params.shape), _const_spec(subln.shape)]
        args += [lam_params, subln]
    return pl.pallas_call(
        functools.partial(_flash_kernel, mode=mode, lam_init=lam_init, chunks_per_iter=TILES_PER_ITER // nmaps),
        grid=(b, npairs, s // TQ),
        in_specs=in_specs,
        out_specs=pl.BlockSpec((None, TQ, LANES), lambda bi, p, i: (bi, i, p)),
        out_shape=jax.ShapeDtypeStruct((b, s, npairs * LANES), BF16),
        scratch_shapes=[pltpu.VMEM((nmaps, 1, TQ), F32), pltpu.VMEM((nmaps, 1, TQ), F32),
                        pltpu.VMEM((nmaps, HEAD_V, TQ), F32),
                        pltpu.VMEM((2, TK, TQ), F32), pltpu.VMEM((2, 1, TQ), F32),
                        pltpu.VMEM((2, TK, TQ), BF16), pltpu.VMEM((2, 1, TQ), F32)],
        compiler_params=pltpu.CompilerParams(dimension_semantics=("arbitrary",) * 3,
                                             vmem_limit_bytes=VMEM_LIMIT),
        name="flash_" + mode,
    )(*args)


def _memkv_kernel(mem_ref, g_ref, w_ref, kv_ref):
    m = _rms_rows(mem_ref[...], g_ref[...]).astype(BF16)
    kv_ref[...] = jnp.dot(m, w_ref[...], preferred_element_type=F32).astype(BF16)


def _memkv(mem, g, w):
    b, t, _ = mem.shape
    return pl.pallas_call(
        _memkv_kernel, grid=(b,),
        in_specs=[pl.BlockSpec((None, t, D_MODEL), lambda bi: (bi, 0, 0)), _const_spec(g.shape), _const_spec(w.shape)],
        out_specs=pl.BlockSpec((None, t, 2 * X_WIDTH), lambda bi: (bi, 0, 0)),
        out_shape=jax.ShapeDtypeStruct((b, t, 2 * X_WIDTH), BF16),
        compiler_params=pltpu.CompilerParams(dimension_semantics=("arbitrary",), vmem_limit_bytes=VMEM_LIMIT),
        name="memkv",
    )(mem, g, w)


def _post_kernel(x_ref, ma_ref, mb_ref, mc_ref, kv_ref, woa_ref, wob_ref, woc_ref, nx_ref, wxq_ref, wxo_ref, o_ref):
    x = x_ref[...]
    x = x + (jnp.dot(ma_ref[...], woa_ref[...], preferred_element_type=F32)
             + jnp.dot(mb_ref[...], wob_ref[...], preferred_element_type=F32)
             + jnp.dot(mc_ref[...], woc_ref[...], preferred_element_type=F32))
    hq = _rms_rows(x, nx_ref[...]).astype(BF16)
    qx = jnp.dot(hq, wxq_ref[...], preferred_element_type=F32).astype(BF16)
    kv = kv_ref[...]
    heads = []
    for hh in range(X_HEADS):
        q = qx[:, X_HD * hh:X_HD * (hh + 1)]
        kx = kv[:, X_HD * hh:X_HD * (hh + 1)]
        vx = kv[:, X_WIDTH + X_HD * hh:X_WIDTH + X_HD * (hh + 1)]
        sc = lax.dot_general(q, kx, NT_DIMS, preferred_element_type=F32) * (X_HD ** -0.5)
        p = jnp.exp(sc - jnp.max(sc, axis=-1, keepdims=True))
        l = jnp.sum(p, axis=-1, keepdims=True)
        heads.append(jnp.dot(p.astype(BF16), vx, preferred_element_type=F32) / l)
    ox = jnp.concatenate(heads, axis=-1).astype(BF16)
    o_ref[...] = x + jnp.dot(ox, wxo_ref[...], preferred_element_type=F32)


def _post(x, ma, mb, mc, kv, woa, wob, woc, nx, wxq, wxo):
    b, s, _ = x.shape
    t = kv.shape[1]
    tile = lambda w: pl.BlockSpec((None, TS, w), lambda bi, i: (bi, i, 0))
    return pl.pallas_call(
        _post_kernel, grid=(b, s // TS),
        in_specs=[tile(D_MODEL), tile(ma.shape[-1]), tile(mb.shape[-1]), tile(mc.shape[-1]),
                  pl.BlockSpec((None, t, 2 * X_WIDTH), lambda bi, i: (bi, 0, 0)),
                  _const_spec(woa.shape), _const_spec(wob.shape), _const_spec(woc.shape),
                  _const_spec(nx.shape), _const_spec(wxq.shape), _const_spec(wxo.shape)],
        out_specs=tile(D_MODEL),
        out_shape=jax.ShapeDtypeStruct(x.shape, F32),
        compiler_params=pltpu.CompilerParams(dimension_semantics=("arbitrary", "arbitrary"),
                                             vmem_limit_bytes=VMEM_LIMIT),
        name="post",
    )(x, ma, mb, mc, kv, woa, wob, woc, nx, wxq, wxo)


def _ffn_kernel(x_ref, nf_ref, wgu_ref, wd_ref, nfin_ref, o_ref, *, final):
    x = x_ref[...]
    hf = _rms_rows(x, nf_ref[...]).astype(BF16)
    gu = jnp.dot(hf, wgu_ref[...], preferred_element_type=F32)
    gate, up = gu[:, :D_FF], gu[:, D_FF:]
    act = (gate / (1.0 + jnp.exp(-gate)) * up).astype(BF16)
    y = x + jnp.dot(act, wd_ref[...], preferred_element_type=F32)
    if final:
        y = _rms_rows(y, nfin_ref[...])
    o_ref[...] = y


def _ffn(x, nf, wgu, wd, nfin, final):
    b, s, _ = x.shape
    tile = pl.BlockSpec((None, TS, D_MODEL), lambda bi, i: (bi, i, 0))
    return pl.pallas_call(
        functools.partial(_ffn_kernel, final=final), grid=(b, s // TS),
        in_specs=[tile, _const_spec(nf.shape), _const_spec(wgu.shape), _const_spec(wd.shape), _const_spec(nfin.shape)],
        out_specs=tile,
        out_shape=jax.ShapeDtypeStruct(x.shape, F32),
        compiler_params=pltpu.CompilerParams(dimension_semantics=("arbitrary", "arbitrary"),
                                             vmem_limit_bytes=VMEM_LIMIT),
        name="ffn",
    )(x, nf, wgu, wd, nfin)


def _rope_table(pos, d, theta):
    half = d // 2
    inv = theta ** (-jnp.arange(half, dtype=F32) * (2.0 / d))
    ang = pos[:, None] * inv[None, :]
    return jnp.cos(ang).T, jnp.sin(ang).T


def _tables(s):
    pos = jnp.arange(s, dtype=F32)
    rows = s // GRID_W
    row_pos = jnp.repeat(jnp.arange(rows, dtype=F32), GRID_W)
    col_pos = jnp.tile(jnp.arange(GRID_W, dtype=F32), rows)
    ca, sa = _rope_table(pos, A_ROT, ROPE_THETA)
    cb, sb = _rope_table(pos, B_ROPE, MLA_THETA)
    cr, sr = _rope_table(row_pos, C_HD // 2, AXIAL_THETA)
    cc, sc = _rope_table(col_pos, C_HD // 2, AXIAL_THETA)
    return jnp.concatenate([ca, ca, -sa, sa, cb, sb, cr, sr, cc, sc], axis=0)


def _encoder(x, mem, tabT, wts):
    for l in range(DEPTH):
        w = wts[l]
        lam_init = 0.8 - 0.6 * math.exp(-0.3 * l)
        qa, ka, va, qb, kb, vb, qc, kc, vc = _proj(x, tabT, w["nmix"], w["winT"], w["gq"], w["wuqT"],
                                                   w["gkv"], w["wukvT"], w["gcq"], w["gck"])
        ma = _flash("A", qa, ka, va, lam_init=lam_init, lam_params=w["lam"], subln=w["subln"])
        mb = _flash("B", qb, kb, vb)
        mc = _flash("C", qc, kc, vc)
        kv = _memkv(mem, w["nmem"], w["wxkv"])
        x = _post(x, ma, mb, mc, kv, w["woa"], w["wob"], w["woc"], w["nx"], w["wxq"], w["wxo"])
        x = _ffn(x, w["nffn"], w["wgu"], w["wd"], w["nfinal"], final=(l == DEPTH - 1))
    return x


def kernel(x_prompt, x_sample, mem_prompt, mem_sample, norm_mix, w_in, lam_q1, lam_k1, lam_q2, lam_k2, subln_a, mla_q_norm, w_uq, mla_kv_norm, w_ukv, c_q_norm, c_k_norm, w_o, norm_x, norm_mem, w_xq, w_xkv, w_xo, norm_ffn, w_gate_up, w_down, norm_final):
    s = x_prompt.shape[1]
    assert x_sample.shape[1] == s and s % TS == 0 and s % TQ == 0 and s % TK == 0 and TS % TK == 0
    tabT = _tables(s)
    row = lambda v: v.reshape(1, -1).astype(F32)
    col = lambda v: v.reshape(-1, 1).astype(F32)
    a_w, b_w = A_HEADS * A_V, B_HEADS * B_V
    wts = []
    for l in range(DEPTH):
        ukv = w_ukv[l].reshape(KV_LORA, B_HEADS, B_NOPE + B_V)
        ukv = jnp.concatenate([ukv[:, :, :B_NOPE].reshape(KV_LORA, -1), ukv[:, :, B_NOPE:].reshape(KV_LORA, -1)], axis=1)
        wts.append(dict(
            nmix=row(norm_mix[l]), winT=w_in[l].T.astype(BF16),
            gq=col(mla_q_norm[l]), wuqT=w_uq[l].T.astype(BF16),
            gkv=col(mla_kv_norm[l]), wukvT=ukv.T.astype(BF16),
            gcq=col(c_q_norm[l]), gck=col(c_k_norm[l]),
            lam=jnp.stack([lam_q1[l], lam_k1[l], lam_q2[l], lam_k2[l]]).astype(F32), subln=col(subln_a[l]),
            woa=w_o[l, :a_w].astype(BF16), wob=w_o[l, a_w:a_w + b_w].astype(BF16), woc=w_o[l, a_w + b_w:].astype(BF16),
            nx=row(norm_x[l]), nmem=row(norm_mem[l]),
            wxq=w_xq[l].astype(BF16), wxkv=w_xkv[l].astype(BF16), wxo=w_xo[l].astype(BF16),
            nffn=row(norm_ffn[l]), wgu=w_gate_up[l].astype(BF16), wd=w_down[l].astype(BF16),
            nfinal=row(norm_final),
        ))
    y_prompt = _encoder(x_prompt, mem_prompt, tabT, wts)
    y_sample = _encoder(x_sample, mem_sample, tabT, wts)
    return (y_prompt, y_sample)
```

```python
import functools
import math

import jax
import jax.numpy as jnp
from jax import lax
from jax.experimental import pallas as pl
from jax.experimental.pallas import tpu as pltpu

F32 = jnp.float32
BF16 = jnp.bfloat16

D_MODEL = 1024
DEPTH = 2
GRID_W = 64
EPS = 1e-6
ROPE_THETA = 500000.0
MLA_THETA = 10000.0
AXIAL_THETA = 10000.0
A_HEADS, A_QK = 4, 32
A_V = 2 * A_QK
A_ROT = A_QK // 4
B_HEADS, B_NOPE, B_ROPE, B_V = 6, 64, 32, 64
Q_LORA, KV_LORA = 256, 128
C_HEADS, C_KV_HEADS, C_HD = 6, 2, 64
X_HEADS, X_HD = 4, 128
X_WIDTH = X_HEADS * X_HD
D_FF = -(-8 * D_MODEL // (3 * 256)) * 256
IN_SIZES = (A_HEADS * 2 * A_QK, A_HEADS * 2 * A_QK, A_HEADS * A_V, Q_LORA, KV_LORA, B_ROPE,
            C_HEADS * C_HD, C_KV_HEADS * C_HD, C_KV_HEADS * C_HD)
IN_COLS = sum(IN_SIZES)
_OFF = [0]
for _n in IN_SIZES:
    _OFF.append(_OFF[-1] + _n)
O_QA, O_KA, O_VA, O_CQ, O_CKV, O_KR, O_QC, O_KC, O_VC, _ = _OFF

LOG2E = math.log2(math.e)
HEAD_V = 64
SUM_ROWS = 16
V_ROWS = HEAD_V + SUM_ROWS
LANES = 128
KD_A, KD_B, KD_C = 128, 256, 128

TS = 512
TQ = 512
TK = 512
TILES_PER_ITER = 8
ROWS = 64
REF_KEYS = 128
F32_MAX = float(jnp.finfo(jnp.float32).max)
VMEM_LIMIT = 56 * 1024 * 1024

NT_DIMS = (((1,), (1,)), ((), ()))

T_CA, T_SA, T_CB, T_SB, T_CR, T_SR, T_CC, T_SC, T_ROWS = 0, 8, 16, 32, 48, 64, 80, 96, 112


def _const_spec(shape):
    zeros = (0,) * len(shape)
    return pl.BlockSpec(shape, lambda *_: zeros, pipeline_mode=pl.Buffered(1))


def _rms_rows(x, g):
    return x * lax.rsqrt(jnp.mean(x * x, axis=-1, keepdims=True) + EPS) * g


def _rms_cols(x, g):
    return x * lax.rsqrt(jnp.mean(x * x, axis=0, keepdims=True) + EPS) * g


def _rot_half(x1, x2, c, s):
    return x1 * c - x2 * s, x1 * s + x2 * c


def _proj_kernel(x_ref, tab_ref, nmix_ref, winT_ref, gq_ref, wuqT_ref, gkv_ref, wukvT_ref,
                 gcq_ref, gck_ref,
                 qa_ref, ka_ref, va_ref, qb_ref, kb_ref, vb_ref, qc_ref, kc_ref, vc_ref):
    ts = x_ref.shape[0]
    nchunk = ts // TK
    h = _rms_rows(x_ref[...], nmix_ref[...]).astype(BF16)
    zT = lax.dot_general(winT_ref[...], h, NT_DIMS, preferred_element_type=F32)
    tab = tab_ref[...]
    cA, sA = tab[T_CA:T_CA + 8], tab[T_SA:T_SA + 8]
    cB, sB = tab[T_CB:T_CB + 16], tab[T_SB:T_SB + 16]
    cR, sR = tab[T_CR:T_CR + 16], tab[T_SR:T_SR + 16]
    cC, sC = tab[T_CC:T_CC + 16], tab[T_SC:T_SC + 16]

    ones_rows = (lax.broadcasted_iota(jnp.int32, (SUM_ROWS, TK), 0) == 0).astype(BF16)

    def store_vt(ref, head, rows):
        for c in range(nchunk):
            ref[head, c, 0:HEAD_V, :] = rows[:, c * TK:(c + 1) * TK].astype(BF16)
            ref[head, c, HEAD_V:V_ROWS, :] = ones_rows

    def rope_a(blk):
        top = blk[0:A_ROT]
        top = top * cA + pltpu.roll(top, A_ROT // 2, 0) * sA
        return jnp.concatenate([top, blk[A_ROT:A_QK]], axis=0)

    scale_a = (A_QK ** -0.5) * LOG2E
    qa_ref[...] = jnp.zeros(qa_ref.shape, BF16)
    ka_blocks = []
    for m in range(2 * A_HEADS):
        q = rope_a(zT[O_QA + A_QK * m:O_QA + A_QK * (m + 1)]) * scale_a
        r = m % 4
        qa_ref[m, A_QK * r:A_QK * (r + 1), :] = q.astype(BF16)
        ka_blocks.append(rope_a(zT[O_KA + A_QK * m:O_KA + A_QK * (m + 1)]))
    ka_ref[...] = jnp.concatenate(ka_blocks, axis=0).T.astype(BF16)
    for hh in range(A_HEADS):
        store_vt(va_ref, hh, zT[O_VA + A_V * hh:O_VA + A_V * (hh + 1)])

    scale_b = ((B_NOPE + B_ROPE) ** -0.5) * LOG2E
    cqn = _rms_cols(zT[O_CQ:O_CQ + Q_LORA], gq_ref[...]).astype(BF16)
    qbT = jnp.dot(wuqT_ref[...], cqn, preferred_element_type=F32)
    qb_ref[...] = jnp.zeros(qb_ref.shape, BF16)
    hd = B_NOPE + B_ROPE
    for hh in range(B_HEADS):
        blk = qbT[hd * hh:hd * (hh + 1)]
        e = hh % 2
        qb_ref[hh, B_NOPE * e:B_NOPE * (e + 1), :] = (blk[0:B_NOPE] * scale_b).astype(BF16)
        r1, r2 = _rot_half(blk[B_NOPE:B_NOPE + 16], blk[B_NOPE + 16:hd], cB, sB)
        qb_ref[hh, 2 * B_NOPE:2 * B_NOPE + B_ROPE, :] = (
            jnp.concatenate([r1, r2], axis=0) * scale_b).astype(BF16)
    ckvn = _rms_cols(zT[O_CKV:O_CKV + KV_LORA], gkv_ref[...]).astype(BF16)
    kvbT = jnp.dot(wukvT_ref[...], ckvn, preferred_element_type=F32)
    kr = zT[O_KR:O_KR + B_ROPE]
    p1, p2 = _rot_half(kr[0:16], kr[16:32], cB, sB)
    peT = jnp.concatenate([p1, p2, jnp.zeros((LANES - B_ROPE, ts), F32)], axis=0)
    pe = peT.T.astype(BF16)
    for p in range(B_HEADS // 2):
        kb_ref[p, :, 0:LANES] = kvbT[LANES * p:LANES * (p + 1)].T.astype(BF16)
        kb_ref[p, :, LANES:2 * LANES] = pe
    for hh in range(B_HEADS):
        store_vt(vb_ref, hh, kvbT[B_HEADS * B_NOPE + B_V * hh:B_HEADS * B_NOPE + B_V * (hh + 1)])

    def norm_rope_c(blk, g):
        y = _rms_cols(blk, g)
        a1, a2 = _rot_half(y[0:16], y[16:32], cR, sR)
        b1, b2 = _rot_half(y[32:48], y[48:64], cC, sC)
        return jnp.concatenate([a1, a2, b1, b2], axis=0)

    scale_c = (C_HD ** -0.5) * LOG2E
    qc_ref[...] = jnp.zeros(qc_ref.shape, BF16)
    rep = C_HEADS // C_KV_HEADS
    for j in range(C_HEADS):
        g = j // rep
        q = norm_rope_c(zT[O_QC + C_HD * j:O_QC + C_HD * (j + 1)], gcq_ref[...]) * scale_c
        qc_ref[j, C_HD * g:C_HD * (g + 1), :] = q.astype(BF16)
    kc = [norm_rope_c(zT[O_KC + C_HD * g:O_KC + C_HD * (g + 1)], gck_ref[...]) for g in range(C_KV_HEADS)]
    kc_ref[...] = jnp.concatenate(kc, axis=0).T.astype(BF16)
    for g in range(C_KV_HEADS):
        store_vt(vc_ref, g, zT[O_VC + C_HD * g:O_VC + C_HD * (g + 1)])


def _proj(x, tabT, nmix, winT, gq, wuqT, gkv, wukvT, gcq, gck):
    b, s, _ = x.shape
    nk = s // TK
    cpt = TS // TK
    out_shape = (
        jax.ShapeDtypeStruct((b, 2 * A_HEADS, KD_A, s), BF16),
        jax.ShapeDtypeStruct((b, s, 2 * A_HEADS * A_QK), BF16),
        jax.ShapeDtypeStruct((b, A_HEADS, nk, V_ROWS, TK), BF16),
        jax.ShapeDtypeStruct((b, B_HEADS, KD_B, s), BF16),
        jax.ShapeDtypeStruct((b, B_HEADS // 2, s, KD_B), BF16),
        jax.ShapeDtypeStruct((b, B_HEADS, nk, V_ROWS, TK), BF16),
        jax.ShapeDtypeStruct((b, C_HEADS, KD_C, s), BF16),
        jax.ShapeDtypeStruct((b, s, C_KV_HEADS * C_HD), BF16),
        jax.ShapeDtypeStruct((b, C_KV_HEADS, nk, V_ROWS, TK), BF16),
    )
    qt_spec = lambda n, kd: pl.BlockSpec((None, n, kd, TS), lambda bi, i: (bi, 0, 0, i))
    vt_spec = lambda n: pl.BlockSpec((None, n, cpt, V_ROWS, TK), lambda bi, i: (bi, 0, i, 0, 0))
    out_specs = (
        qt_spec(2 * A_HEADS, KD_A),
        pl.BlockSpec((None, TS, 2 * A_HEADS * A_QK), lambda bi, i: (bi, i, 0)),
        vt_spec(A_HEADS),
        qt_spec(B_HEADS, KD_B),
        pl.BlockSpec((None, B_HEADS // 2, TS, KD_B), lambda bi, i: (bi, 0, i, 0)),
        vt_spec(B_HEADS),
        qt_spec(C_HEADS, KD_C),
        pl.BlockSpec((None, TS, C_KV_HEADS * C_HD), lambda bi, i: (bi, i, 0)),
        vt_spec(C_KV_HEADS),
    )
    in_specs = [
        pl.BlockSpec((None, TS, D_MODEL), lambda bi, i: (bi, i, 0)),
        pl.BlockSpec((T_ROWS, TS), lambda bi, i: (0, i)),
        _const_spec(nmix.shape), _const_spec(winT.shape), _const_spec(gq.shape), _const_spec(wuqT.shape),
        _const_spec(gkv.shape), _const_spec(wukvT.shape), _const_spec(gcq.shape), _const_spec(gck.shape),
    ]
    return pl.pallas_call(
        _proj_kernel, grid=(b, s // TS), in_specs=in_specs, out_specs=out_specs, out_shape=out_shape,
        compiler_params=pltpu.CompilerParams(dimension_semantics=("arbitrary", "arbitrary"),
                                             vmem_limit_bytes=VMEM_LIMIT),
        name="proj",
    )(x, tabT, nmix, winT, gq, wuqT, gkv, wukvT, gcq, gck)


def _flash_kernel(*refs, mode, lam_init, chunks_per_iter):
    if mode == "A":
        qt_ref, k_ref, vt_ref, lam_ref, sub_ref, o_ref, m_ref, acc_ref, p_scr, s_scr, x_scr, a_scr = refs
    else:
        qt_ref, k_ref, vt_ref, o_ref, m_ref, acc_ref, p_scr, s_scr, x_scr, a_scr = refs
    nmaps = qt_ref.shape[0]
    nk = k_ref.shape[0] // TK
    cpi = chunks_per_iter
    tiles = cpi * nmaps
    if mode == "A":
        vidx = (0, 0, 1, 1)
    elif mode == "B":
        vidx = (0, 1)
    else:
        pair = pl.program_id(1)
        vidx = (pair // 2, (pair + 1) // 2)

    def tile(it, i):
        if i == tiles:
            return jnp.minimum(it * cpi + cpi, nk - 1), 0
        if i < 0:
            return jnp.maximum(it * cpi - 1, 0), nmaps - 1
        return it * cpi + i // nmaps, i % nmaps

    def scores(chunk, mi):
        k_blk = k_ref[pl.ds(pl.multiple_of(chunk * TK, TK), TK), :]
        return jnp.dot(k_blk, qt_ref[mi], preferred_element_type=F32)

    def pv(chunk, mi, slot):
        return jnp.dot(vt_ref[vidx[mi], chunk], p_scr[slot], preferred_element_type=F32)

    for mi in range(nmaps):
        s0 = jnp.dot(k_ref[0:REF_KEYS, :], qt_ref[mi], preferred_element_type=F32)
        m_ref[mi] = jnp.max(s0, axis=0, keepdims=True)
    acc_ref[...] = jnp.zeros(acc_ref.shape, F32)

    def probs(chunk, mi, slot):
        p_scr[slot] = jnp.exp2(scores(chunk, mi) - m_ref[mi]).astype(BF16)

    probs(0, 0, 0)

    def fast_body(it, carry):
        for i in range(tiles):
            probs(*tile(it, i + 1), 1 - i % 2)
            chunk, mi = tile(it, i)
            acc_ref[mi] += pv(chunk, mi, i % 2)
        return carry

    lax.fori_loop(0, nk // cpi, fast_body, 0)
    overflowed = jnp.sum(jnp.where(jnp.abs(acc_ref[...]) <= F32_MAX, 0.0, 1.0)) > 0.0

    @pl.when(overflowed)
    def _():
        def score_tile(chunk, mi, slot):
            s = scores(chunk, mi)
            s_scr[slot] = s
            x_scr[slot] = jnp.max(s, axis=0, keepdims=True)

        def softmax_update(mi, slot):
            m_old = m_ref[mi]
            m_new = jnp.maximum(m_old, x_scr[slot])
            for r in range(TK // ROWS):
                p = jnp.exp2(s_scr[slot, r * ROWS:(r + 1) * ROWS, :] - m_new)
                p_scr[slot, r * ROWS:(r + 1) * ROWS, :] = p.astype(BF16)
            a_scr[slot] = jnp.exp2(m_old - m_new)
            m_ref[mi] = m_new

        def accumulate(chunk, mi, slot):
            acc_ref[mi] = acc_ref[mi] * a_scr[slot] + pv(chunk, mi, slot)

        m_ref[...] = jnp.full(m_ref.shape, -jnp.inf, F32)
        acc_ref[...] = jnp.zeros(acc_ref.shape, F32)
        score_tile(0, 0, 0)
        p_scr[1] = jnp.zeros(p_scr.shape[1:], BF16)
        a_scr[1] = jnp.ones(a_scr.shape[1:], F32)

        def exact_body(it, carry):
            for i in range(tiles):
                score_tile(*tile(it, i + 1), 1 - i % 2)
                accumulate(*tile(it, i - 1), 1 - i % 2)
                softmax_update(i % nmaps, i % 2)
            return carry

        lax.fori_loop(0, nk // cpi, exact_body, 0)
        accumulate(nk - 1, nmaps - 1, 1)

    def head_out(mi):
        acc = acc_ref[mi]
        return acc[0:HEAD_V] / acc[HEAD_V:HEAD_V + 1]

    if mode == "A":
        lp = lam_ref[...]
        lam = (jnp.exp(jnp.sum(lp[0:1] * lp[1:2], axis=-1, keepdims=True))
               - jnp.exp(jnp.sum(lp[2:3] * lp[3:4], axis=-1, keepdims=True)) + lam_init)
        outs = [_rms_cols(head_out(2 * e) - lam * head_out(2 * e + 1), sub_ref[...]) * (1.0 - lam_init)
                for e in range(2)]
    else:
        outs = [head_out(e) for e in range(2)]
    o_ref[...] = jnp.concatenate(outs, axis=0).T.astype(BF16)


def _flash(mode, qt, k, vt, *, lam_init=0.0, lam_params=None, subln=None):
    b, nmaps_total, kd, s = qt.shape
    npairs = {"A": A_HEADS // 2, "B": B_HEADS // 2, "C": C_HEADS // 2}[mode]
    nmaps = nmaps_total // npairs
    nk = s // TK
    qt_spec = pl.BlockSpec((None, nmaps, kd, TQ), lambda bi, p, i: (bi, p, 0, i))
    if mode == "A":
        k_spec = pl.BlockSpec((None, s, kd), lambda bi, p, i: (bi, 0, p))
        vt_spec = pl.BlockSpec((None, 2, nk, V_ROWS, TK), lambda bi, p, i: (bi, p, 0, 0, 0))
    elif mode == "B":
        k_spec = pl.BlockSpec((None, None, s, kd), lambda bi, p, i: (bi, p, 0, 0))
        vt_spec = pl.BlockSpec((None, 2, nk, V_ROWS, TK), lambda bi, p, i: (bi, p, 0, 0, 0))
    else:
        k_spec = pl.BlockSpec((None, s, kd), lambda bi, p, i: (bi, 0, 0))
        vt_spec = pl.BlockSpec((None, C_KV_HEADS, nk, V_ROWS, TK), lambda bi, p, i: (bi, 0, 0, 0, 0))
    in_specs = [qt_spec, k_spec, vt_spec]
    args = [qt, k, vt]
    if mode == "A":
        in_specs += [_const_spec(lam_params.shape), _const_spec(subln.shape)]
        args += [lam_params, subln]
    return pl.pallas_call(
        functools.partial(_flash_kernel, mode=mode, lam_init=lam_init, chunks_per_iter=TILES_PER_ITER // nmaps),
        grid=(b, npairs, s // TQ),
        in_specs=in_specs,
        out_specs=pl.BlockSpec((None, TQ, LANES), lambda bi, p, i: (bi, i, p)),
        out_shape=jax.ShapeDtypeStruct((b, s, npairs * LANES), BF16),
        scratch_shapes=[pltpu.VMEM((nmaps, 1, TQ), F32),
                        pltpu.VMEM((nmaps, V_ROWS, TQ), F32),
                        pltpu.VMEM((2, TK, TQ), BF16),
                        pltpu.VMEM((2, TK, TQ), F32), pltpu.VMEM((2, 1, TQ), F32),
                        pltpu.VMEM((2, 1, TQ), F32)],
        compiler_params=pltpu.CompilerParams(dimension_semantics=("arbitrary",) * 3,
                                             vmem_limit_bytes=VMEM_LIMIT),
        name="flash_" + mode,
    )(*args)


def _memkv_kernel(mem_ref, g_ref, w_ref, kv_ref):
    m = _rms_rows(mem_ref[...], g_ref[...]).astype(BF16)
    kv_ref[...] = jnp.dot(m, w_ref[...], preferred_element_type=F32).astype(BF16)


def _memkv(mem, g, w):
    b, t, _ = mem.shape
    return pl.pallas_call(
        _memkv_kernel, grid=(b,),
        in_specs=[pl.BlockSpec((None, t, D_MODEL), lambda bi: (bi, 0, 0)), _const_spec(g.shape), _const_spec(w.shape)],
        out_specs=pl.BlockSpec((None, t, 2 * X_WIDTH), lambda bi: (bi, 0, 0)),
        out_shape=jax.ShapeDtypeStruct((b, t, 2 * X_WIDTH), BF16),
        compiler_params=pltpu.CompilerParams(dimension_semantics=("arbitrary",), vmem_limit_bytes=VMEM_LIMIT),
        name="memkv",
    )(mem, g, w)


def _post_kernel(x_ref, ma_ref, mb_ref, mc_ref, kv_ref, woa_ref, wob_ref, woc_ref, nx_ref, wxq_ref, wxo_ref, o_ref):
    x = x_ref[...]
    x = x + (jnp.dot(ma_ref[...], woa_ref[...], preferred_element_type=F32)
             + jnp.dot(mb_ref[...], wob_ref[...], preferred_element_type=F32)
             + jnp.dot(mc_ref[...], woc_ref[...], preferred_element_type=F32))
    hq = _rms_rows(x, nx_ref[...]).astype(BF16)
    qx = jnp.dot(hq, wxq_ref[...], preferred_element_type=F32).astype(BF16)
    kv = kv_ref[...]
    heads = []
    for hh in range(X_HEADS):
        q = qx[:, X_HD * hh:X_HD * (hh + 1)]
        kx = kv[:, X_HD * hh:X_HD * (hh + 1)]
        vx = kv[:, X_WIDTH + X_HD * hh:X_WIDTH + X_HD * (hh + 1)]
        sc = lax.dot_general(q, kx, NT_DIMS, preferred_element_type=F32) * (X_HD ** -0.5)
        p = jnp.exp(sc - jnp.max(sc, axis=-1, keepdims=True))
        l = jnp.sum(p, axis=-1, keepdims=True)
        heads.append(jnp.dot(p.astype(BF16), vx, preferred_element_type=F32) / l)
    ox = jnp.concatenate(heads, axis=-1).astype(BF16)
    o_ref[...] = x + jnp.dot(ox, wxo_ref[...], preferred_element_type=F32)


def _post(x, ma, mb, mc, kv, woa, wob, woc, nx, wxq, wxo):
    b, s, _ = x.shape
    t = kv.shape[1]
    tile = lambda w: pl.BlockSpec((None, TS, w), lambda bi, i: (bi, i, 0))
    return pl.pallas_call(
        _post_kernel, grid=(b, s // TS),
        in_specs=[tile(D_MODEL), tile(ma.shape[-1]), tile(mb.shape[-1]), tile(mc.shape[-1]),
                  pl.BlockSpec((None, t, 2 * X_WIDTH), lambda bi, i: (bi, 0, 0)),
                  _const_spec(woa.shape), _const_spec(wob.shape), _const_spec(woc.shape),
                  _const_spec(nx.shape), _const_spec(wxq.shape), _const_spec(wxo.shape)],
        out_specs=tile(D_MODEL),
        out_shape=jax.ShapeDtypeStruct(x.shape, F32),
        compiler_params=pltpu.CompilerParams(dimension_semantics=("arbitrary", "arbitrary"),
                                             vmem_limit_bytes=VMEM_LIMIT),
        name="post",
    )(x, ma, mb, mc, kv, woa, wob, woc, nx, wxq, wxo)


def _ffn_kernel(x_ref, nf_ref, wgu_ref, wd_ref, nfin_ref, o_ref, *, final):
    x = x_ref[...]
    hf = _rms_rows(x, nf_ref[...]).astype(BF16)
    gu = jnp.dot(hf, wgu_ref[...], preferred_element_type=F32)
    gate, up = gu[:, :D_FF], gu[:, D_FF:]
    act = (gate / (1.0 + jnp.exp(-gate)) * up).astype(BF16)
    y = x + jnp.dot(act, wd_ref[...], preferred_element_type=F32)
    if final:
        y = _rms_rows(y, nfin_ref[...])
    o_ref[...] = y


def _ffn(x, nf, wgu, wd, nfin, final):
    b, s, _ = x.shape
    tile = pl.BlockSpec((None, TS, D_MODEL), lambda bi, i: (bi, i, 0))
    return pl.pallas_call(
        functools.partial(_ffn_kernel, final=final), grid=(b, s // TS),
        in_specs=[tile, _const_spec(nf.shape), _const_spec(wgu.shape), _const_spec(wd.shape), _const_spec(nfin.shape)],
        out_specs=tile,
        out_shape=jax.ShapeDtypeStruct(x.shape, F32),
        compiler_params=pltpu.CompilerParams(dimension_semantics=("arbitrary", "arbitrary"),
                                             vmem_limit_bytes=VMEM_LIMIT),
        name="ffn",
    )(x, nf, wgu, wd, nfin)


def _rope_table(pos, d, theta):
    half = d // 2
    inv = theta ** (-jnp.arange(half, dtype=F32) * (2.0 / d))
    ang = pos[:, None] * inv[None, :]
    return jnp.cos(ang).T, jnp.sin(ang).T


def _tables(s):
    pos = jnp.arange(s, dtype=F32)
    rows = s // GRID_W
    row_pos = jnp.repeat(jnp.arange(rows, dtype=F32), GRID_W)
    col_pos = jnp.tile(jnp.arange(GRID_W, dtype=F32), rows)
    ca, sa = _rope_table(pos, A_ROT, ROPE_THETA)
    cb, sb = _rope_table(pos, B_ROPE, MLA_THETA)
    cr, sr = _rope_table(row_pos, C_HD // 2, AXIAL_THETA)
    cc, sc = _rope_table(col_pos, C_HD // 2, AXIAL_THETA)
    return jnp.concatenate([ca, ca, -sa, sa, cb, sb, cr, sr, cc, sc], axis=0)


def _encoder(x, mem, tabT, wts):
    for l in range(DEPTH):
        w = wts[l]
        lam_init = 0.8 - 0.6 * math.exp(-0.3 * l)
        qa, ka, va, qb, kb, vb, qc, kc, vc = _proj(x, tabT, w["nmix"], w["winT"], w["gq"], w["wuqT"],
                                                   w["gkv"], w["wukvT"], w["gcq"], w["gck"])
        ma = _flash("A", qa, ka, va, lam_init=lam_init, lam_params=w["lam"], subln=w["subln"])
        mb = _flash("B", qb, kb, vb)
        mc = _flash("C", qc, kc, vc)
        kv = _memkv(mem, w["nmem"], w["wxkv"])
        x = _post(x, ma, mb, mc, kv, w["woa"], w["wob"], w["woc"], w["nx"], w["wxq"], w["wxo"])
        x = _ffn(x, w["nffn"], w["wgu"], w["wd"], w["nfinal"], final=(l == DEPTH - 1))
    return x


def kernel(x_prompt, x_sample, mem_prompt, mem_sample, norm_mix, w_in, lam_q1, lam_k1, lam_q2, lam_k2, subln_a, mla_q_norm, w_uq, mla_kv_norm, w_ukv, c_q_norm, c_k_norm, w_o, norm_x, norm_mem, w_xq, w_xkv, w_xo, norm_ffn, w_gate_up, w_down, norm_final):
    s = x_prompt.shape[1]
    assert x_sample.shape[1] == s and s % TS == 0 and s % TQ == 0 and s % TK == 0 and TS % TK == 0
    tabT = _tables(s)
    row = lambda v: v.reshape(1, -1).astype(F32)
    col = lambda v: v.reshape(-1, 1).astype(F32)
    a_w, b_w = A_HEADS * A_V, B_HEADS * B_V
    wts = []
    for l in range(DEPTH):
        ukv = w_ukv[l].reshape(KV_LORA, B_HEADS, B_NOPE + B_V)
        ukv = jnp.concatenate([ukv[:, :, :B_NOPE].reshape(KV_LORA, -1), ukv[:, :, B_NOPE:].reshape(KV_LORA, -1)], axis=1)
        wts.append(dict(
            nmix=row(norm_mix[l]), winT=w_in[l].T.astype(BF16),
            gq=col(mla_q_norm[l]), wuqT=w_uq[l].T.astype(BF16),
            gkv=col(mla_kv_norm[l]), wukvT=ukv.T.astype(BF16),
            gcq=col(c_q_norm[l]), gck=col(c_k_norm[l]),
            lam=jnp.stack([lam_q1[l], lam_k1[l], lam_q2[l], lam_k2[l]]).astype(F32), subln=col(subln_a[l]),
            woa=w_o[l, :a_w].astype(BF16), wob=w_o[l, a_w:a_w + b_w].astype(BF16), woc=w_o[l, a_w + b_w:].astype(BF16),
            nx=row(norm_x[l]), nmem=row(norm_mem[l]),
            wxq=w_xq[l].astype(BF16), wxkv=w_xkv[l].astype(BF16), wxo=w_xo[l].astype(BF16),
            nffn=row(norm_ffn[l]), wgu=w_gate_up[l].astype(BF16), wd=w_down[l].astype(BF16),
            nfinal=row(norm_final),
        ))
    y_prompt = _encoder(x_prompt, mem_prompt, tabT, wts)
    y_sample = _encoder(x_sample, mem_sample, tabT, wts)
    return (y_prompt, y_sample)
```

```python
import functools
import math

import jax
import jax.numpy as jnp
from jax import lax
from jax.experimental import pallas as pl
from jax.experimental.pallas import tpu as pltpu

F32 = jnp.float32
BF16 = jnp.bfloat16

D_MODEL = 1024
DEPTH = 2
GRID_W = 64
EPS = 1e-6
ROPE_THETA = 500000.0
MLA_THETA = 10000.0
AXIAL_THETA = 10000.0
A_HEADS, A_QK = 4, 32
A_V = 2 * A_QK
A_ROT = A_QK // 4
B_HEADS, B_NOPE, B_ROPE, B_V = 6, 64, 32, 64
Q_LORA, KV_LORA = 256, 128
C_HEADS, C_KV_HEADS, C_HD = 6, 2, 64
X_HEADS, X_HD = 4, 128
X_WIDTH = X_HEADS * X_HD
D_FF = -(-8 * D_MODEL // (3 * 256)) * 256
IN_SIZES = (A_HEADS * 2 * A_QK, A_HEADS * 2 * A_QK, A_HEADS * A_V, Q_LORA, KV_LORA, B_ROPE,
            C_HEADS * C_HD, C_KV_HEADS * C_HD, C_KV_HEADS * C_HD)
IN_COLS = sum(IN_SIZES)
_OFF = [0]
for _n in IN_SIZES:
    _OFF.append(_OFF[-1] + _n)
O_QA, O_KA, O_VA, O_CQ, O_CKV, O_KR, O_QC, O_KC, O_VC, _ = _OFF

LOG2E = math.log2(math.e)
HEAD_V = 64
SUM_ROWS = 16
V_ROWS = HEAD_V + SUM_ROWS
LANES = 128
KD_A, KD_B, KD_C = 128, 256, 128

TS = 512
TQ = 1024
TK = 256
TILES_PER_ITER = 16
ROWS = 64
REF_KEYS = 128
F32_MAX = float(jnp.finfo(jnp.float32).max)
VMEM_LIMIT = 56 * 1024 * 1024

NT_DIMS = (((1,), (1,)), ((), ()))

T_CA, T_SA, T_CB, T_SB, T_CR, T_SR, T_CC, T_SC, T_ROWS = 0, 8, 16, 32, 48, 64, 80, 96, 112


def _const_spec(shape):
    zeros = (0,) * len(shape)
    return pl.BlockSpec(shape, lambda *_: zeros, pipeline_mode=pl.Buffered(1))


def _rms_rows(x, g):
    return x * lax.rsqrt(jnp.mean(x * x, axis=-1, keepdims=True) + EPS) * g


def _rms_cols(x, g):
    return x * lax.rsqrt(jnp.mean(x * x, axis=0, keepdims=True) + EPS) * g


def _rot_half(x1, x2, c, s):
    return x1 * c - x2 * s, x1 * s + x2 * c


def _proj_kernel(x_ref, tab_ref, nmix_ref, winT_ref, gq_ref, wuqT_ref, gkv_ref, wukvT_ref,
                 gcq_ref, gck_ref,
                 qa_ref, ka_ref, va_ref, qb_ref, kb_ref, vb_ref, qc_ref, kc_ref, vc_ref):
    ts = x_ref.shape[0]
    nchunk = ts // TK
    h = _rms_rows(x_ref[...], nmix_ref[...]).astype(BF16)
    zT = lax.dot_general(winT_ref[...], h, NT_DIMS, preferred_element_type=F32)
    tab = tab_ref[...]
    cA, sA = tab[T_CA:T_CA + 8], tab[T_SA:T_SA + 8]
    cB, sB = tab[T_CB:T_CB + 16], tab[T_SB:T_SB + 16]
    cR, sR = tab[T_CR:T_CR + 16], tab[T_SR:T_SR + 16]
    cC, sC = tab[T_CC:T_CC + 16], tab[T_SC:T_SC + 16]

    ones_rows = (lax.broadcasted_iota(jnp.int32, (SUM_ROWS, TK), 0) == 0).astype(BF16)

    def store_vt(ref, head, rows):
        for c in range(nchunk):
            ref[head, c, 0:HEAD_V, :] = rows[:, c * TK:(c + 1) * TK].astype(BF16)
            ref[head, c, HEAD_V:V_ROWS, :] = ones_rows

    def rope_a(blk):
        top = blk[0:A_ROT]
        top = top * cA + pltpu.roll(top, A_ROT // 2, 0) * sA
        return jnp.concatenate([top, blk[A_ROT:A_QK]], axis=0)

    scale_a = (A_QK ** -0.5) * LOG2E
    qa_ref[...] = jnp.zeros(qa_ref.shape, BF16)
    ka_blocks = []
    for m in range(2 * A_HEADS):
        q = rope_a(zT[O_QA + A_QK * m:O_QA + A_QK * (m + 1)]) * scale_a
        r = m % 4
        qa_ref[m, A_QK * r:A_QK * (r + 1), :] = q.astype(BF16)
        ka_blocks.append(rope_a(zT[O_KA + A_QK * m:O_KA + A_QK * (m + 1)]))
    ka_ref[...] = jnp.concatenate(ka_blocks, axis=0).T.astype(BF16)
    for hh in range(A_HEADS):
        store_vt(va_ref, hh, zT[O_VA + A_V * hh:O_VA + A_V * (hh + 1)])

    scale_b = ((B_NOPE + B_ROPE) ** -0.5) * LOG2E
    cqn = _rms_cols(zT[O_CQ:O_CQ + Q_LORA], gq_ref[...]).astype(BF16)
    qbT = jnp.dot(wuqT_ref[...], cqn, preferred_element_type=F32)
    qb_ref[...] = jnp.zeros(qb_ref.shape, BF16)
    hd = B_NOPE + B_ROPE
    for hh in range(B_HEADS):
        blk = qbT[hd * hh:hd * (hh + 1)]
        e = hh % 2
        qb_ref[hh, B_NOPE * e:B_NOPE * (e + 1), :] = (blk[0:B_NOPE] * scale_b).astype(BF16)
        r1, r2 = _rot_half(blk[B_NOPE:B_NOPE + 16], blk[B_NOPE + 16:hd], cB, sB)
        qb_ref[hh, 2 * B_NOPE:2 * B_NOPE + B_ROPE, :] = (
            jnp.concatenate([r1, r2], axis=0) * scale_b).astype(BF16)
    ckvn = _rms_cols(zT[O_CKV:O_CKV + KV_LORA], gkv_ref[...]).astype(BF16)
    kvbT = jnp.dot(wukvT_ref[...], ckvn, preferred_element_type=F32)
    kr = zT[O_KR:O_KR + B_ROPE]
    p1, p2 = _rot_half(kr[0:16], kr[16:32], cB, sB)
    peT = jnp.concatenate([p1, p2, jnp.zeros((LANES - B_ROPE, ts), F32)], axis=0)
    pe = peT.T.astype(BF16)
    for p in range(B_HEADS // 2):
        kb_ref[p, :, 0:LANES] = kvbT[LANES * p:LANES * (p + 1)].T.astype(BF16)
        kb_ref[p, :, LANES:2 * LANES] = pe
    for hh in range(B_HEADS):
        store_vt(vb_ref, hh, kvbT[B_HEADS * B_NOPE + B_V * hh:B_HEADS * B_NOPE + B_V * (hh + 1)])

    def norm_rope_c(blk, g):
        y = _rms_cols(blk, g)
        a1, a2 = _rot_half(y[0:16], y[16:32], cR, sR)
        b1, b2 = _rot_half(y[32:48], y[48:64], cC, sC)
        return jnp.concatenate([a1, a2, b1, b2], axis=0)

    scale_c = (C_HD ** -0.5) * LOG2E
    qc_ref[...] = jnp.zeros(qc_ref.shape, BF16)
    rep = C_HEADS // C_KV_HEADS
    for j in range(C_HEADS):
        g = j // rep
        q = norm_rope_c(zT[O_QC + C_HD * j:O_QC + C_HD * (j + 1)], gcq_ref[...]) * scale_c
        qc_ref[j, C_HD * g:C_HD * (g + 1), :] = q.astype(BF16)
    kc = [norm_rope_c(zT[O_KC + C_HD * g:O_KC + C_HD * (g + 1)], gck_ref[...]) for g in range(C_KV_HEADS)]
    kc_ref[...] = jnp.concatenate(kc, axis=0).T.astype(BF16)
    for g in range(C_KV_HEADS):
        store_vt(vc_ref, g, zT[O_VC + C_HD * g:O_VC + C_HD * (g + 1)])


def _proj(x, tabT, nmix, winT, gq, wuqT, gkv, wukvT, gcq, gck):
    b, s, _ = x.shape
    nk = s // TK
    cpt = TS // TK
    out_shape = (
        jax.ShapeDtypeStruct((b, 2 * A_HEADS, KD_A, s), BF16),
        jax.ShapeDtypeStruct((b, s, 2 * A_HEADS * A_QK), BF16),
        jax.ShapeDtypeStruct((b, A_HEADS, nk, V_ROWS, TK), BF16),
        jax.ShapeDtypeStruct((b, B_HEADS, KD_B, s), BF16),
        jax.ShapeDtypeStruct((b, B_HEADS // 2, s, KD_B), BF16),
        jax.ShapeDtypeStruct((b, B_HEADS, nk, V_ROWS, TK), BF16),
        jax.ShapeDtypeStruct((b, C_HEADS, KD_C, s), BF16),
        jax.ShapeDtypeStruct((b, s, C_KV_HEADS * C_HD), BF16),
        jax.ShapeDtypeStruct((b, C_KV_HEADS, nk, V_ROWS, TK), BF16),
    )
    qt_spec = lambda n, kd: pl.BlockSpec((None, n, kd, TS), lambda bi, i: (bi, 0, 0, i))
    vt_spec = lambda n: pl.BlockSpec((None, n, cpt, V_ROWS, TK), lambda bi, i: (bi, 0, i, 0, 0))
    out_specs = (
        qt_spec(2 * A_HEADS, KD_A),
        pl.BlockSpec((None, TS, 2 * A_HEADS * A_QK), lambda bi, i: (bi, i, 0)),
        vt_spec(A_HEADS),
        qt_spec(B_HEADS, KD_B),
        pl.BlockSpec((None, B_HEADS // 2, TS, KD_B), lambda bi, i: (bi, 0, i, 0)),
        vt_spec(B_HEADS),
        qt_spec(C_HEADS, KD_C),
        pl.BlockSpec((None, TS, C_KV_HEADS * C_HD), lambda bi, i: (bi, i, 0)),
        vt_spec(C_KV_HEADS),
    )
    in_specs = [
        pl.BlockSpec((None, TS, D_MODEL), lambda bi, i: (bi, i, 0)),
        pl.BlockSpec((T_ROWS, TS), lambda bi, i: (0, i)),
        _const_spec(nmix.shape), _const_spec(winT.shape), _const_spec(gq.shape), _const_spec(wuqT.shape),
        _const_spec(gkv.shape), _const_spec(wukvT.shape), _const_spec(gcq.shape), _const_spec(gck.shape),
    ]
    return pl.pallas_call(
        _proj_kernel, grid=(b, s // TS), in_specs=in_specs, out_specs=out_specs, out_shape=out_shape,
        compiler_params=pltpu.CompilerParams(dimension_semantics=("arbitrary", "arbitrary"),
                                             vmem_limit_bytes=VMEM_LIMIT),
        name="proj",
    )(x, tabT, nmix, winT, gq, wuqT, gkv, wukvT, gcq, gck)


def _flash_kernel(*refs, mode, lam_init, chunks_per_iter):
    if mode == "A":
        qt_ref, k_ref, vt_ref, lam_ref, sub_ref, o_ref, m_ref, acc_ref, p_scr, s_scr, x_scr, a_scr = refs
    else:
        qt_ref, k_ref, vt_ref, o_ref, m_ref, acc_ref, p_scr, s_scr, x_scr, a_scr = refs
    nmaps = qt_ref.shape[0]
    nk = k_ref.shape[0] // TK
    cpi = chunks_per_iter
    tiles = cpi * nmaps
    if mode == "A":
        vidx = (0, 0, 1, 1)
    elif mode == "B":
        vidx = (0, 1)
    else:
        pair = pl.program_id(1)
        vidx = (pair // 2, (pair + 1) // 2)

    def tile(it, i):
        if i == tiles:
            return jnp.minimum(it * cpi + cpi, nk - 1), 0
        if i < 0:
            return jnp.maximum(it * cpi - 1, 0), nmaps - 1
        return it * cpi + i // nmaps, i % nmaps

    def scores(chunk, mi):
        k_blk = k_ref[pl.ds(pl.multiple_of(chunk * TK, TK), TK), :]
        return jnp.dot(k_blk, qt_ref[mi], preferred_element_type=F32)

    def pv(chunk, mi, slot):
        return jnp.dot(vt_ref[vidx[mi], chunk], p_scr[slot], preferred_element_type=F32)

    for mi in range(nmaps):
        s0 = jnp.dot(k_ref[0:REF_KEYS, :], qt_ref[mi], preferred_element_type=F32)
        m_ref[mi] = jnp.max(s0, axis=0, keepdims=True)
    acc_ref[...] = jnp.zeros(acc_ref.shape, F32)

    def probs(chunk, mi, slot):
        p_scr[slot] = jnp.exp2(scores(chunk, mi) - m_ref[mi]).astype(BF16)

    probs(0, 0, 0)

    def fast_body(it, carry):
        for i in range(tiles):
            probs(*tile(it, i + 1), 1 - i % 2)
            chunk, mi = tile(it, i)
            acc_ref[mi] += pv(chunk, mi, i % 2)
        return carry

    lax.fori_loop(0, nk // cpi, fast_body, 0)

    def finalize():
        def head_out(mi):
            acc = acc_ref[mi]
            return acc[0:HEAD_V] / acc[HEAD_V:HEAD_V + 1]

        if mode == "A":
            lp = lam_ref[...]
            lam = (jnp.exp(jnp.sum(lp[0:1] * lp[1:2], axis=-1, keepdims=True))
                   - jnp.exp(jnp.sum(lp[2:3] * lp[3:4], axis=-1, keepdims=True)) + lam_init)
            outs = [_rms_cols(head_out(2 * e) - lam * head_out(2 * e + 1), sub_ref[...]) * (1.0 - lam_init)
                    for e in range(2)]
        else:
            outs = [head_out(e) for e in range(2)]
        return jnp.concatenate(outs, axis=0)

    out = finalize()
    o_ref[...] = out.T.astype(BF16)
    sums = acc_ref[:, HEAD_V:HEAD_V + 1, :]
    overflowed = (jnp.sum(jnp.where(sums <= F32_MAX, 0.0, 1.0))
                  + jnp.sum(jnp.where(jnp.abs(out) <= F32_MAX, 0.0, 1.0))) > 0.0

    @pl.when(overflowed)
    def _():
        def score_tile(chunk, mi, slot):
            s = scores(chunk, mi)
            s_scr[slot] = s
            x_scr[slot] = jnp.max(s, axis=0, keepdims=True)

        def softmax_update(mi, slot):
            m_old = m_ref[mi]
            m_new = jnp.maximum(m_old, x_scr[slot])
            for r in range(TK // ROWS):
                p = jnp.exp2(s_scr[slot, r * ROWS:(r + 1) * ROWS, :] - m_new)
                p_scr[slot, r * ROWS:(r + 1) * ROWS, :] = p.astype(BF16)
            a_scr[slot] = jnp.exp2(m_old - m_new)
            m_ref[mi] = m_new

        def accumulate(chunk, mi, slot):
            acc_ref[mi] = acc_ref[mi] * a_scr[slot] + pv(chunk, mi, slot)

        m_ref[...] = jnp.full(m_ref.shape, -jnp.inf, F32)
        acc_ref[...] = jnp.zeros(acc_ref.shape, F32)
        score_tile(0, 0, 0)
        p_scr[1] = jnp.zeros(p_scr.shape[1:], BF16)
        a_scr[1] = jnp.ones(a_scr.shape[1:], F32)

        def exact_body(it, carry):
            for i in range(tiles):
                score_tile(*tile(it, i + 1), 1 - i % 2)
                accumulate(*tile(it, i - 1), 1 - i % 2)
                softmax_update(i % nmaps, i % 2)
            return carry

        lax.fori_loop(0, nk // cpi, exact_body, 0)
        accumulate(nk - 1, nmaps - 1, 1)
        o_ref[...] = finalize().T.astype(BF16)


def _flash(mode, qt, k, vt, *, lam_init=0.0, lam_params=None, subln=None):
    b, nmaps_total, kd, s = qt.shape
    npairs = {"A": A_HEADS // 2, "B": B_HEADS // 2, "C": C_HEADS // 2}[mode]
    nmaps = nmaps_total // npairs
    nk = s // TK
    assert TILES_PER_ITER % (2 * nmaps) == 0 and nk % (TILES_PER_ITER // nmaps) == 0
    qt_spec = pl.BlockSpec((None, nmaps, kd, TQ), lambda bi, p, i: (bi, p, 0, i))
    if mode == "A":
        k_spec = pl.BlockSpec((None, s, kd), lambda bi, p, i: (bi, 0, p))
        vt_spec = pl.BlockSpec((None, 2, nk, V_ROWS, TK), lambda bi, p, i: (bi, p, 0, 0, 0))
    elif mode == "B":
        k_spec = pl.BlockSpec((None, None, s, kd), lambda bi, p, i: (bi, p, 0, 0))
        vt_spec = pl.BlockSpec((None, 2, nk, V_ROWS, TK), lambda bi, p, i: (bi, p, 0, 0, 0))
    else:
        k_spec = pl.BlockSpec((None, s, kd), lambda bi, p, i: (bi, 0, 0))
        vt_spec = pl.BlockSpec((None, C_KV_HEADS, nk, V_ROWS, TK), lambda bi, p, i: (bi, 0, 0, 0, 0))
    in_specs = [qt_spec, k_spec, vt_spec]
    args = [qt, k, vt]
    if mode == "A":
        in_specs += [_const_spec(lam_params.shape), _const_spec(subln.shape)]
        args += [lam_params, subln]
    return pl.pallas_call(
        functools.partial(_flash_kernel, mode=mode, lam_init=lam_init, chunks_per_iter=TILES_PER_ITER // nmaps),
        grid=(b, npairs, s // TQ),
        in_specs=in_specs,
        out_specs=pl.BlockSpec((None, TQ, LANES), lambda bi, p, i: (bi, i, p)),
        out_shape=jax.ShapeDtypeStruct((b, s, npairs * LANES), BF16),
        scratch_shapes=[pltpu.VMEM((nmaps, 1, TQ), F32),
                        pltpu.VMEM((nmaps, V_ROWS, TQ), F32),
                        pltpu.VMEM((2, TK, TQ), BF16),
                        pltpu.VMEM((2, TK, TQ), F32), pltpu.VMEM((2, 1, TQ), F32),
                        pltpu.VMEM((2, 1, TQ), F32)],
        compiler_params=pltpu.CompilerParams(dimension_semantics=("arbitrary",) * 3,
                                             vmem_limit_bytes=VMEM_LIMIT),
        name="flash_" + mode,
    )(*args)


def _memkv_kernel(mem_ref, g_ref, w_ref, kv_ref):
    m = _rms_rows(mem_ref[...], g_ref[...]).astype(BF16)
    kv_ref[...] = jnp.dot(m, w_ref[...], preferred_element_type=F32).astype(BF16)


def _memkv(mem, g, w):
    b, t, _ = mem.shape
    return pl.pallas_call(
        _memkv_kernel, grid=(b,),
        in_specs=[pl.BlockSpec((None, t, D_MODEL), lambda bi: (bi, 0, 0)), _const_spec(g.shape), _const_spec(w.shape)],
        out_specs=pl.BlockSpec((None, t, 2 * X_WIDTH), lambda bi: (bi, 0, 0)),
        out_shape=jax.ShapeDtypeStruct((b, t, 2 * X_WIDTH), BF16),
        compiler_params=pltpu.CompilerParams(dimension_semantics=("arbitrary",), vmem_limit_bytes=VMEM_LIMIT),
        name="memkv",
    )(mem, g, w)


def _post_kernel(x_ref, ma_ref, mb_ref, mc_ref, kv_ref, woa_ref, wob_ref, woc_ref, nx_ref, wxq_ref, wxo_ref, o_ref):
    x = x_ref[...]
    x = x + (jnp.dot(ma_ref[...], woa_ref[...], preferred_element_type=F32)
             + jnp.dot(mb_ref[...], wob_ref[...], preferred_element_type=F32)
             + jnp.dot(mc_ref[...], woc_ref[...], preferred_element_type=F32))
    hq = _rms_rows(x, nx_ref[...]).astype(BF16)
    qx = jnp.dot(hq, wxq_ref[...], preferred_element_type=F32).astype(BF16)
    kv = kv_ref[...]
    heads = []
    for hh in range(X_HEADS):
        q = qx[:, X_HD * hh:X_HD * (hh + 1)]
        kx = kv[:, X_HD * hh:X_HD * (hh + 1)]
        vx = kv[:, X_WIDTH + X_HD * hh:X_WIDTH + X_HD * (hh + 1)]
        sc = lax.dot_general(q, kx, NT_DIMS, preferred_element_type=F32) * (X_HD ** -0.5)
        p = jnp.exp(sc - jnp.max(sc, axis=-1, keepdims=True))
        l = jnp.sum(p, axis=-1, keepdims=True)
        heads.append(jnp.dot(p.astype(BF16), vx, preferred_element_type=F32) / l)
    ox = jnp.concatenate(heads, axis=-1).astype(BF16)
    o_ref[...] = x + jnp.dot(ox, wxo_ref[...], preferred_element_type=F32)


def _post(x, ma, mb, mc, kv, woa, wob, woc, nx, wxq, wxo):
    b, s, _ = x.shape
    t = kv.shape[1]
    tile = lambda w: pl.BlockSpec((None, TS, w), lambda bi, i: (bi, i, 0))
    return pl.pallas_call(
        _post_kernel, grid=(b, s // TS),
        in_specs=[tile(D_MODEL), tile(ma.shape[-1]), tile(mb.shape[-1]), tile(mc.shape[-1]),
                  pl.BlockSpec((None, t, 2 * X_WIDTH), lambda bi, i: (bi, 0, 0)),
                  _const_spec(woa.shape), _const_spec(wob.shape), _const_spec(woc.shape),
                  _const_spec(nx.shape), _const_spec(wxq.shape), _const_spec(wxo.shape)],
        out_specs=tile(D_MODEL),
        out_shape=jax.ShapeDtypeStruct(x.shape, F32),
        compiler_params=pltpu.CompilerParams(dimension_semantics=("arbitrary", "arbitrary"),
                                             vmem_limit_bytes=VMEM_LIMIT),
        name="post",
    )(x, ma, mb, mc, kv, woa, wob, woc, nx, wxq, wxo)


def _ffn_kernel(x_ref, nf_ref, wgu_ref, wd_ref, nfin_ref, o_ref, *, final):
    x = x_ref[...]
    hf = _rms_rows(x, nf_ref[...]).astype(BF16)
    gu = jnp.dot(hf, wgu_ref[...], preferred_element_type=F32)
    gate, up = gu[:, :D_FF], gu[:, D_FF:]
    act = (gate / (1.0 + jnp.exp(-gate)) * up).astype(BF16)
    y = x + jnp.dot(act, wd_ref[...], preferred_element_type=F32)
    if final:
        y = _rms_rows(y, nfin_ref[...])
    o_ref[...] = y


def _ffn(x, nf, wgu, wd, nfin, final):
    b, s, _ = x.shape
    tile = pl.BlockSpec((None, TS, D_MODEL), lambda bi, i: (bi, i, 0))
    return pl.pallas_call(
        functools.partial(_ffn_kernel, final=final), grid=(b, s // TS),
        in_specs=[tile, _const_spec(nf.shape), _const_spec(wgu.shape), _const_spec(wd.shape), _const_spec(nfin.shape)],
        out_specs=tile,
        out_shape=jax.ShapeDtypeStruct(x.shape, F32),
        compiler_params=pltpu.CompilerParams(dimension_semantics=("arbitrary", "arbitrary"),
                                             vmem_limit_bytes=VMEM_LIMIT),
        name="ffn",
    )(x, nf, wgu, wd, nfin)


def _rope_table(pos, d, theta):
    half = d // 2
    inv = theta ** (-jnp.arange(half, dtype=F32) * (2.0 / d))
    ang = pos[:, None] * inv[None, :]
    return jnp.cos(ang).T, jnp.sin(ang).T


def _tables(s):
    pos = jnp.arange(s, dtype=F32)
    rows = s // GRID_W
    row_pos = jnp.repeat(jnp.arange(rows, dtype=F32), GRID_W)
    col_pos = jnp.tile(jnp.arange(GRID_W, dtype=F32), rows)
    ca, sa = _rope_table(pos, A_ROT, ROPE_THETA)
    cb, sb = _rope_table(pos, B_ROPE, MLA_THETA)
    cr, sr = _rope_table(row_pos, C_HD // 2, AXIAL_THETA)
    cc, sc = _rope_table(col_pos, C_HD // 2, AXIAL_THETA)
    return jnp.concatenate([ca, ca, -sa, sa, cb, sb, cr, sr, cc, sc], axis=0)


def _encoder(x, mem, tabT, wts):
    for l in range(DEPTH):
        w = wts[l]
        lam_init = 0.8 - 0.6 * math.exp(-0.3 * l)
        qa, ka, va, qb, kb, vb, qc, kc, vc = _proj(x, tabT, w["nmix"], w["winT"], w["gq"], w["wuqT"],
                                                   w["gkv"], w["wukvT"], w["gcq"], w["gck"])
        ma = _flash("A", qa, ka, va, lam_init=lam_init, lam_params=w["lam"], subln=w["subln"])
        mb = _flash("B", qb, kb, vb)
        mc = _flash("C", qc, kc, vc)
        kv = _memkv(mem, w["nmem"], w["wxkv"])
        x = _post(x, ma, mb, mc, kv, w["woa"], w["wob"], w["woc"], w["nx"], w["wxq"], w["wxo"])
        x = _ffn(x, w["nffn"], w["wgu"], w["wd"], w["nfinal"], final=(l == DEPTH - 1))
    return x


def kernel(x_prompt, x_sample, mem_prompt, mem_sample, norm_mix, w_in, lam_q1, lam_k1, lam_q2, lam_k2, subln_a, mla_q_norm, w_uq, mla_kv_norm, w_ukv, c_q_norm, c_k_norm, w_o, norm_x, norm_mem, w_xq, w_xkv, w_xo, norm_ffn, w_gate_up, w_down, norm_final):
    s = x_prompt.shape[1]
    assert x_sample.shape[1] == s and s % TS == 0 and s % TQ == 0 and s % TK == 0 and TS % TK == 0
    tabT = _tables(s)
    row = lambda v: v.reshape(1, -1).astype(F32)
    col = lambda v: v.reshape(-1, 1).astype(F32)
    a_w, b_w = A_HEADS * A_V, B_HEADS * B_V
    wts = []
    for l in range(DEPTH):
        ukv = w_ukv[l].reshape(KV_LORA, B_HEADS, B_NOPE + B_V)
        ukv = jnp.concatenate([ukv[:, :, :B_NOPE].reshape(KV_LORA, -1), ukv[:, :, B_NOPE:].reshape(KV_LORA, -1)], axis=1)
        wts.append(dict(
            nmix=row(norm_mix[l]), winT=w_in[l].T.astype(BF16),
            gq=col(mla_q_norm[l]), wuqT=w_uq[l].T.astype(BF16),
            gkv=col(mla_kv_norm[l]), wukvT=ukv.T.astype(BF16),
            gcq=col(c_q_norm[l]), gck=col(c_k_norm[l]),
            lam=jnp.stack([lam_q1[l], lam_k1[l], lam_q2[l], lam_k2[l]]).astype(F32), subln=col(subln_a[l]),
            woa=w_o[l, :a_w].astype(BF16), wob=w_o[l, a_w:a_w + b_w].astype(BF16), woc=w_o[l, a_w + b_w:].astype(BF16),
            nx=row(norm_x[l]), nmem=row(norm_mem[l]),
            wxq=w_xq[l].astype(BF16), wxkv=w_xkv[l].astype(BF16), wxo=w_xo[l].astype(BF16),
            nffn=row(norm_ffn[l]), wgu=w_gate_up[l].astype(BF16), wd=w_down[l].astype(BF16),
            nfinal=row(norm_final),
        ))
    y_prompt = _encoder(x_prompt, mem_prompt, tabT, wts)
    y_sample = _encoder(x_sample, mem_sample, tabT, wts)
    return (y_prompt, y_sample)
```

```python
import functools
import math

import jax
import jax.numpy as jnp
from jax import lax
from jax.experimental import pallas as pl
from jax.experimental.pallas import tpu as pltpu

F32 = jnp.float32
BF16 = jnp.bfloat16

D_MODEL = 1024
DEPTH = 2
GRID_W = 64
EPS = 1e-6
ROPE_THETA = 500000.0
MLA_THETA = 10000.0
AXIAL_THETA = 10000.0
A_HEADS, A_QK = 4, 32
A_V = 2 * A_QK
A_ROT = A_QK // 4
B_HEADS, B_NOPE, B_ROPE, B_V = 6, 64, 32, 64
Q_LORA, KV_LORA = 256, 128
C_HEADS, C_KV_HEADS, C_HD = 6, 2, 64
X_HEADS, X_HD = 4, 128
X_WIDTH = X_HEADS * X_HD
D_FF = -(-8 * D_MODEL // (3 * 256)) * 256
IN_SIZES = (A_HEADS * 2 * A_QK, A_HEADS * 2 * A_QK, A_HEADS * A_V, Q_LORA, KV_LORA, B_ROPE,
            C_HEADS * C_HD, C_KV_HEADS * C_HD, C_KV_HEADS * C_HD)
IN_COLS = sum(IN_SIZES)
_OFF = [0]
for _n in IN_SIZES:
    _OFF.append(_OFF[-1] + _n)
O_QA, O_KA, O_VA, O_CQ, O_CKV, O_KR, O_QC, O_KC, O_VC, _ = _OFF

LOG2E = math.log2(math.e)
HEAD_V = 64
SUM_ROWS = 16
V_ROWS = HEAD_V + SUM_ROWS
LANES = 128
KD_A, KD_B, KD_C = 128, 256, 128

TS = 512
TQ = 1024
TK = 256
TILES_PER_ITER = 32
ROWS = 64
REF_KEYS = 32
F32_MAX = float(jnp.finfo(jnp.float32).max)
VMEM_LIMIT = 56 * 1024 * 1024

NT_DIMS = (((1,), (1,)), ((), ()))

T_CA, T_SA, T_CB, T_SB, T_CR, T_SR, T_CC, T_SC, T_ROWS = 0, 8, 16, 32, 48, 64, 80, 96, 112


def _const_spec(shape):
    zeros = (0,) * len(shape)
    return pl.BlockSpec(shape, lambda *_: zeros, pipeline_mode=pl.Buffered(1))


def _rms_rows(x, g):
    return x * lax.rsqrt(jnp.mean(x * x, axis=-1, keepdims=True) + EPS) * g


def _rms_cols(x, g):
    return x * lax.rsqrt(jnp.mean(x * x, axis=0, keepdims=True) + EPS) * g


def _rot_half(x1, x2, c, s):
    return x1 * c - x2 * s, x1 * s + x2 * c


def _proj_kernel(x_ref, tab_ref, nmix_ref, winT_ref, gq_ref, wuqT_ref, gkv_ref, wukvT_ref,
                 gcq_ref, gck_ref,
                 qa_ref, ka_ref, va_ref, qb_ref, kb_ref, vb_ref, qc_ref, kc_ref, vc_ref):
    ts = x_ref.shape[0]
    h = _rms_rows(x_ref[...], nmix_ref[...]).astype(BF16)
    zT = lax.dot_general(winT_ref[...], h, NT_DIMS, preferred_element_type=F32)
    tab = tab_ref[...]
    cA, sA = tab[T_CA:T_CA + 8], tab[T_SA:T_SA + 8]
    cB, sB = tab[T_CB:T_CB + 16], tab[T_SB:T_SB + 16]
    cR, sR = tab[T_CR:T_CR + 16], tab[T_SR:T_SR + 16]
    cC, sC = tab[T_CC:T_CC + 16], tab[T_SC:T_SC + 16]

    def store_vt(ref, head, rows):
        tk = ref.shape[-1]
        ones_rows = (lax.broadcasted_iota(jnp.int32, (SUM_ROWS, tk), 0) == 0).astype(BF16)
        for c in range(ts // tk):
            ref[head, c, 0:HEAD_V, :] = rows[:, c * tk:(c + 1) * tk].astype(BF16)
            ref[head, c, HEAD_V:V_ROWS, :] = ones_rows

    def rope_a(blk):
        top = blk[0:A_ROT]
        top = top * cA + pltpu.roll(top, A_ROT // 2, 0) * sA
        return jnp.concatenate([top, blk[A_ROT:A_QK]], axis=0)

    scale_a = (A_QK ** -0.5) * LOG2E
    qa_ref[...] = jnp.zeros(qa_ref.shape, BF16)
    ka_blocks = []
    for m in range(2 * A_HEADS):
        q = rope_a(zT[O_QA + A_QK * m:O_QA + A_QK * (m + 1)]) * scale_a
        r = m % 4
        qa_ref[m, A_QK * r:A_QK * (r + 1), :] = q.astype(BF16)
        ka_blocks.append(rope_a(zT[O_KA + A_QK * m:O_KA + A_QK * (m + 1)]))
    ka_ref[...] = jnp.concatenate(ka_blocks, axis=0).T.astype(BF16)
    for hh in range(A_HEADS):
        store_vt(va_ref, hh, zT[O_VA + A_V * hh:O_VA + A_V * (hh + 1)])

    scale_b = ((B_NOPE + B_ROPE) ** -0.5) * LOG2E
    cqn = _rms_cols(zT[O_CQ:O_CQ + Q_LORA], gq_ref[...]).astype(BF16)
    qbT = jnp.dot(wuqT_ref[...], cqn, preferred_element_type=F32)
    qb_ref[...] = jnp.zeros(qb_ref.shape, BF16)
    hd = B_NOPE + B_ROPE
    for hh in range(B_HEADS):
        blk = qbT[hd * hh:hd * (hh + 1)]
        e = hh % 2
        qb_ref[hh, B_NOPE * e:B_NOPE * (e + 1), :] = (blk[0:B_NOPE] * scale_b).astype(BF16)
        r1, r2 = _rot_half(blk[B_NOPE:B_NOPE + 16], blk[B_NOPE + 16:hd], cB, sB)
        qb_ref[hh, 2 * B_NOPE:2 * B_NOPE + B_ROPE, :] = (
            jnp.concatenate([r1, r2], axis=0) * scale_b).astype(BF16)
    ckvn = _rms_cols(zT[O_CKV:O_CKV + KV_LORA], gkv_ref[...]).astype(BF16)
    kvbT = jnp.dot(wukvT_ref[...], ckvn, preferred_element_type=F32)
    kr = zT[O_KR:O_KR + B_ROPE]
    p1, p2 = _rot_half(kr[0:16], kr[16:32], cB, sB)
    peT = jnp.concatenate([p1, p2, jnp.zeros((LANES - B_ROPE, ts), F32)], axis=0)
    pe = peT.T.astype(BF16)
    for p in range(B_HEADS // 2):
        kb_ref[p, :, 0:LANES] = kvbT[LANES * p:LANES * (p + 1)].T.astype(BF16)
        kb_ref[p, :, LANES:2 * LANES] = pe
    for hh in range(B_HEADS):
        store_vt(vb_ref, hh, kvbT[B_HEADS * B_NOPE + B_V * hh:B_HEADS * B_NOPE + B_V * (hh + 1)])

    def norm_rope_c(blk, g):
        y = _rms_cols(blk, g)
        a1, a2 = _rot_half(y[0:16], y[16:32], cR, sR)
        b1, b2 = _rot_half(y[32:48], y[48:64], cC, sC)
        return jnp.concatenate([a1, a2, b1, b2], axis=0)

    scale_c = (C_HD ** -0.5) * LOG2E
    qc_ref[...] = jnp.zeros(qc_ref.shape, BF16)
    rep = C_HEADS // C_KV_HEADS
    for j in range(C_HEADS):
        g = j // rep
        q = norm_rope_c(zT[O_QC + C_HD * j:O_QC + C_HD * (j + 1)], gcq_ref[...]) * scale_c
        qc_ref[j, C_HD * g:C_HD * (g + 1), :] = q.astype(BF16)
    kc = [norm_rope_c(zT[O_KC + C_HD * g:O_KC + C_HD * (g + 1)], gck_ref[...]) for g in range(C_KV_HEADS)]
    kc_ref[...] = jnp.concatenate(kc, axis=0).T.astype(BF16)
    for g in range(C_KV_HEADS):
        store_vt(vc_ref, g, zT[O_VC + C_HD * g:O_VC + C_HD * (g + 1)])


def _proj(x, tabT, nmix, winT, gq, wuqT, gkv, wukvT, gcq, gck):
    b, s, _ = x.shape
    vt_shape = lambda n: jax.ShapeDtypeStruct((b, n, s // TK, V_ROWS, TK), BF16)
    out_shape = (
        jax.ShapeDtypeStruct((b, 2 * A_HEADS, KD_A, s), BF16),
        jax.ShapeDtypeStruct((b, s, 2 * A_HEADS * A_QK), BF16),
        vt_shape(A_HEADS),
        jax.ShapeDtypeStruct((b, B_HEADS, KD_B, s), BF16),
        jax.ShapeDtypeStruct((b, B_HEADS // 2, s, KD_B), BF16),
        vt_shape(B_HEADS),
        jax.ShapeDtypeStruct((b, C_HEADS, KD_C, s), BF16),
        jax.ShapeDtypeStruct((b, s, C_KV_HEADS * C_HD), BF16),
        vt_shape(C_KV_HEADS),
    )
    qt_spec = lambda n, kd: pl.BlockSpec((None, n, kd, TS), lambda bi, i: (bi, 0, 0, i))
    vt_spec = lambda n: pl.BlockSpec((None, n, TS // TK, V_ROWS, TK), lambda bi, i: (bi, 0, i, 0, 0))
    out_specs = (
        qt_spec(2 * A_HEADS, KD_A),
        pl.BlockSpec((None, TS, 2 * A_HEADS * A_QK), lambda bi, i: (bi, i, 0)),
        vt_spec(A_HEADS),
        qt_spec(B_HEADS, KD_B),
        pl.BlockSpec((None, B_HEADS // 2, TS, KD_B), lambda bi, i: (bi, 0, i, 0)),
        vt_spec(B_HEADS),
        qt_spec(C_HEADS, KD_C),
        pl.BlockSpec((None, TS, C_KV_HEADS * C_HD), lambda bi, i: (bi, i, 0)),
        vt_spec(C_KV_HEADS),
    )
    in_specs = [
        pl.BlockSpec((None, TS, D_MODEL), lambda bi, i: (bi, i, 0)),
        pl.BlockSpec((T_ROWS, TS), lambda bi, i: (0, i)),
        _const_spec(nmix.shape), _const_spec(winT.shape), _const_spec(gq.shape), _const_spec(wuqT.shape),
        _const_spec(gkv.shape), _const_spec(wukvT.shape), _const_spec(gcq.shape), _const_spec(gck.shape),
    ]
    return pl.pallas_call(
        _proj_kernel, grid=(b, s // TS), in_specs=in_specs, out_specs=out_specs, out_shape=out_shape,
        compiler_params=pltpu.CompilerParams(dimension_semantics=("arbitrary", "arbitrary"),
                                             vmem_limit_bytes=VMEM_LIMIT),
        name="proj",
    )(x, tabT, nmix, winT, gq, wuqT, gkv, wukvT, gcq, gck)


def _flash_kernel(*refs, mode, lam_init, chunks_per_iter):
    if mode == "A":
        qt_ref, k_ref, vt_ref, lam_ref, sub_ref, o_ref, m_ref, acc_ref, p_scr, s_scr, x_scr, a_scr = refs
    else:
        qt_ref, k_ref, vt_ref, o_ref, m_ref, acc_ref, p_scr, s_scr, x_scr, a_scr = refs
    nmaps = qt_ref.shape[0]
    tk = p_scr.shape[1]
    nk = k_ref.shape[0] // tk
    cpi = chunks_per_iter
    tiles = cpi * nmaps
    if mode == "A":
        vidx = (0, 0, 1, 1)
    elif mode == "B":
        vidx = (0, 1)
    else:
        pair = pl.program_id(1)
        vidx = (pair // 2, (pair + 1) // 2)

    def tile(it, i):
        if i == tiles:
            return jnp.minimum(it * cpi + cpi, nk - 1), 0
        if i < 0:
            return jnp.maximum(it * cpi - 1, 0), nmaps - 1
        return it * cpi + i // nmaps, i % nmaps

    def scores(chunk, mi):
        k_blk = k_ref[pl.ds(pl.multiple_of(chunk * tk, tk), tk), :]
        return jnp.dot(k_blk, qt_ref[mi], preferred_element_type=F32)

    def pv(chunk, mi, slot):
        return jnp.dot(vt_ref[vidx[mi], chunk], p_scr[slot], preferred_element_type=F32)

    for mi in range(nmaps):
        s0 = jnp.dot(k_ref[0:REF_KEYS, :], qt_ref[mi], preferred_element_type=F32)
        m_ref[mi] = jnp.max(s0, axis=0, keepdims=True)
    acc_ref[...] = jnp.zeros(acc_ref.shape, F32)

    def probs(chunk, mi, slot):
        p_scr[slot] = jnp.exp2(scores(chunk, mi) - m_ref[mi]).astype(BF16)

    probs(0, 0, 0)

    def fast_body(it, carry):
        for i in range(tiles):
            probs(*tile(it, i + 1), 1 - i % 2)
            chunk, mi = tile(it, i)
            acc_ref[mi] += pv(chunk, mi, i % 2)
        return carry

    lax.fori_loop(0, nk // cpi, fast_body, 0)

    def finalize():
        def head_out(mi):
            acc = acc_ref[mi]
            return acc[0:HEAD_V] * (1.0 / acc[HEAD_V:HEAD_V + 1])

        if mode == "A":
            lp = lam_ref[...]
            lam = (jnp.exp(jnp.sum(lp[0:1] * lp[1:2], axis=-1, keepdims=True))
                   - jnp.exp(jnp.sum(lp[2:3] * lp[3:4], axis=-1, keepdims=True)) + lam_init)
            outs = [_rms_cols(head_out(2 * e) - lam * head_out(2 * e + 1), sub_ref[...]) * (1.0 - lam_init)
                    for e in range(2)]
        else:
            outs = [head_out(e) for e in range(2)]
        return jnp.concatenate(outs, axis=0)

    out = finalize()
    o_ref[...] = out.T.astype(BF16)
    sums = acc_ref[:, HEAD_V:HEAD_V + 1, :]
    overflowed = (jnp.sum(jnp.where(sums <= F32_MAX, 0.0, 1.0))
                  + jnp.sum(jnp.where(jnp.abs(out) <= F32_MAX, 0.0, 1.0))) > 0.0

    @pl.when(overflowed)
    def _():
        def score_tile(chunk, mi, slot):
            s = scores(chunk, mi)
            s_scr[slot] = s
            x_scr[slot] = jnp.max(s, axis=0, keepdims=True)

        def softmax_update(mi, slot):
            m_old = m_ref[mi]
            m_new = jnp.maximum(m_old, x_scr[slot])
            for r in range(tk // ROWS):
                p = jnp.exp2(s_scr[slot, r * ROWS:(r + 1) * ROWS, :] - m_new)
                p_scr[slot, r * ROWS:(r + 1) * ROWS, :] = p.astype(BF16)
            a_scr[slot] = jnp.exp2(m_old - m_new)
            m_ref[mi] = m_new

        def accumulate(chunk, mi, slot):
            acc_ref[mi] = acc_ref[mi] * a_scr[slot] + pv(chunk, mi, slot)

        m_ref[...] = jnp.full(m_ref.shape, -jnp.inf, F32)
        acc_ref[...] = jnp.zeros(acc_ref.shape, F32)
        score_tile(0, 0, 0)
        p_scr[1] = jnp.zeros(p_scr.shape[1:], BF16)
        a_scr[1] = jnp.ones(a_scr.shape[1:], F32)

        def exact_body(it, carry):
            for i in range(tiles):
                score_tile(*tile(it, i + 1), 1 - i % 2)
                accumulate(*tile(it, i - 1), 1 - i % 2)
                softmax_update(i % nmaps, i % 2)
            return carry

        lax.fori_loop(0, nk // cpi, exact_body, 0)
        accumulate(nk - 1, nmaps - 1, 1)
        o_ref[...] = finalize().T.astype(BF16)


def _flash(mode, qt, k, vt, *, lam_init=0.0, lam_params=None, subln=None):
    b, nmaps_total, kd, s = qt.shape
    npairs = {"A": A_HEADS // 2, "B": B_HEADS // 2, "C": C_HEADS // 2}[mode]
    nmaps = nmaps_total // npairs
    tk = TK
    nk = s // tk
    assert TILES_PER_ITER % (2 * nmaps) == 0 and nk % (TILES_PER_ITER // nmaps) == 0
    qt_spec = pl.BlockSpec((None, nmaps, kd, TQ), lambda bi, p, i: (bi, p, 0, i))
    if mode == "A":
        k_spec = pl.BlockSpec((None, s, kd), lambda bi, p, i: (bi, 0, p))
        vt_spec = pl.BlockSpec((None, 2, nk, V_ROWS, tk), lambda bi, p, i: (bi, p, 0, 0, 0))
    elif mode == "B":
        k_spec = pl.BlockSpec((None, None, s, kd), lambda bi, p, i: (bi, p, 0, 0))
        vt_spec = pl.BlockSpec((None, 2, nk, V_ROWS, tk), lambda bi, p, i: (bi, p, 0, 0, 0))
    else:
        k_spec = pl.BlockSpec((None, s, kd), lambda bi, p, i: (bi, 0, 0))
        vt_spec = pl.BlockSpec((None, C_KV_HEADS, nk, V_ROWS, tk), lambda bi, p, i: (bi, 0, 0, 0, 0))
    in_specs = [qt_spec, k_spec, vt_spec]
    args = [qt, k, vt]
    if mode == "A":
        in_specs += [_const_spec(lam_params.shape), _const_spec(subln.shape)]
        args += [lam_params, subln]
    return pl.pallas_call(
        functools.partial(_flash_kernel, mode=mode, lam_init=lam_init, chunks_per_iter=TILES_PER_ITER // nmaps),
        grid=(b, npairs, s // TQ),
        in_specs=in_specs,
        out_specs=pl.BlockSpec((None, TQ, LANES), lambda bi, p, i: (bi, i, p)),
        out_shape=jax.ShapeDtypeStruct((b, s, npairs * LANES), BF16),
        scratch_shapes=[pltpu.VMEM((nmaps, 1, TQ), F32),
                        pltpu.VMEM((nmaps, V_ROWS, TQ), F32),
                        pltpu.VMEM((2, tk, TQ), BF16),
                        pltpu.VMEM((2, tk, TQ), F32), pltpu.VMEM((2, 1, TQ), F32),
                        pltpu.VMEM((2, 1, TQ), F32)],
        compiler_params=pltpu.CompilerParams(dimension_semantics=("arbitrary",) * 3,
                                             vmem_limit_bytes=VMEM_LIMIT),
        name="flash_" + mode,
    )(*args)


def _memkv_kernel(mem_ref, g_ref, w_ref, kv_ref):
    m = _rms_rows(mem_ref[...], g_ref[...]).astype(BF16)
    kv_ref[...] = jnp.dot(m, w_ref[...], preferred_element_type=F32).astype(BF16)


def _memkv(mem, g, w):
    b, t, _ = mem.shape
    return pl.pallas_call(
        _memkv_kernel, grid=(b,),
        in_specs=[pl.BlockSpec((None, t, D_MODEL), lambda bi: (bi, 0, 0)), _const_spec(g.shape), _const_spec(w.shape)],
        out_specs=pl.BlockSpec((None, t, 2 * X_WIDTH), lambda bi: (bi, 0, 0)),
        out_shape=jax.ShapeDtypeStruct((b, t, 2 * X_WIDTH), BF16),
        compiler_params=pltpu.CompilerParams(dimension_semantics=("arbitrary",), vmem_limit_bytes=VMEM_LIMIT),
        name="memkv",
    )(mem, g, w)


def _post_ffn_kernel(x_ref, ma_ref, mb_ref, mc_ref, kv_ref, woa_ref, wob_ref, woc_ref, nx_ref, wxq_ref, wxo_ref,
                     nf_ref, wgu_ref, wd_ref, nfin_ref, o_ref, *, final):
    x = x_ref[...]
    x = x + (jnp.dot(ma_ref[...], woa_ref[...], preferred_element_type=F32)
             + jnp.dot(mb_ref[...], wob_ref[...], preferred_element_type=F32)
             + jnp.dot(mc_ref[...], woc_ref[...], preferred_element_type=F32))
    hq = _rms_rows(x, nx_ref[...]).astype(BF16)
    qx = jnp.dot(hq, wxq_ref[...], preferred_element_type=F32).astype(BF16)
    kv = kv_ref[...]
    heads = []
    for hh in range(X_HEADS):
        q = qx[:, X_HD * hh:X_HD * (hh + 1)]
        kx = kv[:, X_HD * hh:X_HD * (hh + 1)]
        vx = kv[:, X_WIDTH + X_HD * hh:X_WIDTH + X_HD * (hh + 1)]
        sc = lax.dot_general(q, kx, NT_DIMS, preferred_element_type=F32) * (X_HD ** -0.5)
        p = jnp.exp(sc - jnp.max(sc, axis=-1, keepdims=True))
        l = jnp.sum(p, axis=-1, keepdims=True)
        heads.append(jnp.dot(p.astype(BF16), vx, preferred_element_type=F32) / l)
    ox = jnp.concatenate(heads, axis=-1).astype(BF16)
    x = x + jnp.dot(ox, wxo_ref[...], preferred_element_type=F32)
    hf = _rms_rows(x, nf_ref[...]).astype(BF16)
    gu = jnp.dot(hf, wgu_ref[...], preferred_element_type=F32)
    gate, up = gu[:, :D_FF], gu[:, D_FF:]
    act = (gate / (1.0 + jnp.exp(-gate)) * up).astype(BF16)
    y = x + jnp.dot(act, wd_ref[...], preferred_element_type=F32)
    if final:
        y = _rms_rows(y, nfin_ref[...])
    o_ref[...] = y


def _post_ffn(x, ma, mb, mc, kv, w, final):
    b, s, _ = x.shape
    t = kv.shape[1]
    tile = lambda width: pl.BlockSpec((None, TS, width), lambda bi, i: (bi, i, 0))
    consts = [w[k] for k in ("woa", "wob", "woc", "nx", "wxq", "wxo", "nffn", "wgu", "wd", "nfinal")]
    return pl.pallas_call(
        functools.partial(_post_ffn_kernel, final=final), grid=(b, s // TS),
        in_specs=[tile(D_MODEL), tile(ma.shape[-1]), tile(mb.shape[-1]), tile(mc.shape[-1]),
                  pl.BlockSpec((None, t, 2 * X_WIDTH), lambda bi, i: (bi, 0, 0))]
                 + [_const_spec(c.shape) for c in consts],
        out_specs=tile(D_MODEL),
        out_shape=jax.ShapeDtypeStruct(x.shape, F32),
        compiler_params=pltpu.CompilerParams(dimension_semantics=("arbitrary", "arbitrary"),
                                             vmem_limit_bytes=VMEM_LIMIT),
        name="post_ffn",
    )(x, ma, mb, mc, kv, *consts)


def _rope_table(pos, d, theta):
    half = d // 2
    inv = theta ** (-jnp.arange(half, dtype=F32) * (2.0 / d))
    ang = pos[:, None] * inv[None, :]
    return jnp.cos(ang).T, jnp.sin(ang).T


def _tables(s):
    pos = jnp.arange(s, dtype=F32)
    rows = s // GRID_W
    row_pos = jnp.repeat(jnp.arange(rows, dtype=F32), GRID_W)
    col_pos = jnp.tile(jnp.arange(GRID_W, dtype=F32), rows)
    ca, sa = _rope_table(pos, A_ROT, ROPE_THETA)
    cb, sb = _rope_table(pos, B_ROPE, MLA_THETA)
    cr, sr = _rope_table(row_pos, C_HD // 2, AXIAL_THETA)
    cc, sc = _rope_table(col_pos, C_HD // 2, AXIAL_THETA)
    return jnp.concatenate([ca, ca, -sa, sa, cb, sb, cr, sr, cc, sc], axis=0)


def _encoder(x, mem, tabT, wts):
    for l in range(DEPTH):
        w = wts[l]
        lam_init = 0.8 - 0.6 * math.exp(-0.3 * l)
        qa, ka, va, qb, kb, vb, qc, kc, vc = _proj(x, tabT, w["nmix"], w["winT"], w["gq"], w["wuqT"],
                                                   w["gkv"], w["wukvT"], w["gcq"], w["gck"])
        ma = _flash("A", qa, ka, va, lam_init=lam_init, lam_params=w["lam"], subln=w["subln"])
        mb = _flash("B", qb, kb, vb)
        mc = _flash("C", qc, kc, vc)
        kv = _memkv(mem, w["nmem"], w["wxkv"])
        x = _post_ffn(x, ma, mb, mc, kv, w, final=(l == DEPTH - 1))
    return x


def kernel(x_prompt, x_sample, mem_prompt, mem_sample, norm_mix, w_in, lam_q1, lam_k1, lam_q2, lam_k2, subln_a, mla_q_norm, w_uq, mla_kv_norm, w_ukv, c_q_norm, c_k_norm, w_o, norm_x, norm_mem, w_xq, w_xkv, w_xo, norm_ffn, w_gate_up, w_down, norm_final):
    s = x_prompt.shape[1]
    assert x_sample.shape[1] == s and s % TS == 0 and s % TQ == 0 and TS % TK == 0
    tabT = _tables(s)
    row = lambda v: v.reshape(1, -1).astype(F32)
    col = lambda v: v.reshape(-1, 1).astype(F32)
    a_w, b_w = A_HEADS * A_V, B_HEADS * B_V
    wts = []
    for l in range(DEPTH):
        ukv = w_ukv[l].reshape(KV_LORA, B_HEADS, B_NOPE + B_V)
        ukv = jnp.concatenate([ukv[:, :, :B_NOPE].reshape(KV_LORA, -1), ukv[:, :, B_NOPE:].reshape(KV_LORA, -1)], axis=1)
        wts.append(dict(
            nmix=row(norm_mix[l]), winT=w_in[l].T.astype(BF16),
            gq=col(mla_q_norm[l]), wuqT=w_uq[l].T.astype(BF16),
            gkv=col(mla_kv_norm[l]), wukvT=ukv.T.astype(BF16),
            gcq=col(c_q_norm[l]), gck=col(c_k_norm[l]),
            lam=jnp.stack([lam_q1[l], lam_k1[l], lam_q2[l], lam_k2[l]]).astype(F32), subln=col(subln_a[l]),
            woa=w_o[l, :a_w].astype(BF16), wob=w_o[l, a_w:a_w + b_w].astype(BF16), woc=w_o[l, a_w + b_w:].astype(BF16),
            nx=row(norm_x[l]), nmem=row(norm_mem[l]),
            wxq=w_xq[l].astype(BF16), wxkv=w_xkv[l].astype(BF16), wxo=w_xo[l].astype(BF16),
            nffn=row(norm_ffn[l]), wgu=w_gate_up[l].astype(BF16), wd=w_down[l].astype(BF16),
            nfinal=row(norm_final),
        ))
    y_prompt = _encoder(x_prompt, mem_prompt, tabT, wts)
    y_sample = _encoder(x_sample, mem_sample, tabT, wts)
    return (y_prompt, y_sample)
```
